```python
import jax, jax.numpy as jnp
from jax import lax
import numpy as np

D_MODEL = 2048
BATCH = 8
SEQ = 2048
DEPTH = 4
DEC_BATCH = 4
DEC_SEQ = 2048
PAST_LEN = 128

N_MIXERS = 3
N_SSD_LAYERS = (DEPTH + 2) // 3
N_FNET_LAYERS = (DEPTH + 1) // 3
N_ATTN_LAYERS = DEPTH // 3
NORM_EPS = 1e-6

SSD_EXPAND = 2
D_INNER = SSD_EXPAND * D_MODEL
SSD_HEAD_DIM = 64
SSD_HEADS = D_INNER // SSD_HEAD_DIM
SSD_GROUPS = 8
SSD_HEADS_PER_GROUP = SSD_HEADS // SSD_GROUPS
SSD_STATE = 128
SSD_CONV = 5
SSD_CHUNK = 128
CONV_DIM = D_INNER + 2 * SSD_GROUPS * SSD_STATE
SSD_IN_DIM = D_INNER + CONV_DIM + 2 * SSD_HEADS

FNET_GROUPS = 4

ATTN_HEAD_DIM = 64
ATTN_Q_HEADS = D_MODEL // ATTN_HEAD_DIM
ATTN_KV_HEADS = 8
ATTN_GROUP = ATTN_Q_HEADS // ATTN_KV_HEADS
WINDOW = 128
ATTN_BLOCK = WINDOW
ROPE_THETA = 500000.0
ROPE_DIM = ATTN_HEAD_DIM // 4
ATTN_IN_DIM = (ATTN_Q_HEADS + 2 * ATTN_KV_HEADS) * ATTN_HEAD_DIM

D_FF = ((8 * D_MODEL + 3 * 256 - 1) // (3 * 256)) * 256

kernel_name = "hybrid_bidir_ssd_fnet_swa_encoder"


def _rms_norm(x, g):
    xf = x.astype(jnp.float32)
    y = xf * lax.rsqrt(jnp.mean(xf * xf, axis=-1, keepdims=True) + NORM_EPS)
    return (y * g.astype(jnp.float32)).astype(x.dtype)


def _centred_depthwise_conv(x, w, bias):
    c = x.shape[-1]
    pad = SSD_CONV // 2
    y = lax.conv_general_dilated(x, w.astype(x.dtype)[:, None, :], (1,), [(pad, pad)],
                                 dimension_numbers=('NWC', 'WIO', 'NWC'),
                                 feature_group_count=c)
    return y + bias.astype(x.dtype)


def _segsum_exp(a_cum):
    t = a_cum.shape[-1]
    diff = a_cum[..., :, None] - a_cum[..., None, :]
    tril = jnp.tril(jnp.ones((t, t), dtype=bool))
    return jnp.where(tril, jnp.exp(jnp.where(tril, diff, 0.0)), 0.0)


def _ssd_chunked_scan(x, dt, a, bm, cm):
    b, seq_len = x.shape[:2]
    g, r, p, n, q = SSD_GROUPS, SSD_HEADS_PER_GROUP, SSD_HEAD_DIM, SSD_STATE, SSD_CHUNK
    c = seq_len // q
    xdt = (x.astype(jnp.float32) * dt.reshape(b, seq_len, g, r, 1)).reshape(b, c, q, g, r, p)
    adt = (dt * a).reshape(b, c, q, g, r).transpose(0, 3, 4, 1, 2)
    bc = bm.astype(jnp.float32).reshape(b, c, q, g, n)
    cc = cm.astype(jnp.float32).reshape(b, c, q, g, n)
    a_cum = jnp.cumsum(adt, axis=-1)
    lmat = _segsum_exp(a_cum).transpose(0, 3, 1, 2, 4, 5)
    cb = jnp.einsum('bclgn,bcsgn->bcgls', cc, bc)
    y_diag = jnp.einsum('bcgrls,bcsgrp->bclgrp', cb[:, :, :, None] * lmat, xdt)
    decay_states = jnp.exp(a_cum[..., -1:] - a_cum).transpose(0, 3, 4, 1, 2)
    states = jnp.einsum('bcsgn,bcsgrp->bcgrpn', bc, xdt * decay_states[..., None])
    chunk_cum = jnp.cumsum(jnp.pad(a_cum[..., -1], ((0, 0), (0, 0), (0, 0), (1, 0))), axis=-1)
    decay_chunk = _segsum_exp(chunk_cum)
    states = jnp.pad(states, ((0, 0), (1, 0), (0, 0), (0, 0), (0, 0), (0, 0)))
    carried = jnp.einsum('bgrzc,bcgrpn->bzgrpn', decay_chunk, states)[:, :-1]
    out_decay = jnp.exp(a_cum).transpose(0, 3, 4, 1, 2)
    y_off = jnp.einsum('bclgn,bcgrpn->bclgrp', cc, carried) * out_decay[..., None]
    return (y_diag + y_off).reshape(b, seq_len, g, r, p)


def _ssd_mixer(h, w_in, conv_w, conv_b, dt_bias, a_log, d_skip, norm_w, w_out):
    b, seq_len, _ = h.shape
    g, r, p, n = SSD_GROUPS, SSD_HEADS_PER_GROUP, SSD_HEAD_DIM, SSD_STATE
    proj = h @ w_in
    z = proj[..., :D_INNER]
    xbc = proj[..., D_INNER:D_INNER + CONV_DIM]
    dt_raw = proj[..., D_INNER + CONV_DIM:].reshape(b, seq_len, 2, SSD_HEADS)
    xbc = jax.nn.silu(_centred_depthwise_conv(xbc, conv_w, conv_b))
    gn = g * n
    xs = xbc[..., :D_INNER].reshape(b, seq_len, g, r, p)
    bm = xbc[..., D_INNER:D_INNER + gn].reshape(b, seq_len, g, n)
    cm = xbc[..., D_INNER + gn:].reshape(b, seq_len, g, n)
    dt = jax.nn.softplus(dt_raw.astype(jnp.float32) + dt_bias.astype(jnp.float32))
    a = -jnp.exp(a_log.astype(jnp.float32))
    flip = lambda t: jnp.flip(t, axis=1)
    y_fwd = _ssd_chunked_scan(xs, dt[:, :, 0], a[0], bm, cm)
    y_bwd = flip(_ssd_chunked_scan(flip(xs), flip(dt[:, :, 1]), a[1], flip(bm), flip(cm)))
    y = y_fwd + y_bwd + d_skip.astype(jnp.float32).reshape(g, r, 1) * xs.astype(jnp.float32)
    y = y.reshape(b, seq_len, D_INNER) * jax.nn.silu(z.astype(jnp.float32))
    yg = y.reshape(b, seq_len, g, D_INNER // g)
    yg = yg * lax.rsqrt(jnp.mean(yg * yg, axis=-1, keepdims=True) + NORM_EPS)
    y = (yg.reshape(b, seq_len, D_INNER) * norm_w.astype(jnp.float32)).astype(h.dtype)
    return y @ w_out


def _fourier_mixer(h, w_out):
    b, seq_len, d = h.shape
    hg = h.astype(jnp.float32).reshape(b, seq_len, FNET_GROUPS, d // FNET_GROUPS)
    f = jnp.fft.fftn(hg, axes=(1, 3), norm='ortho').real
    return f.reshape(b, seq_len, d).astype(h.dtype) @ w_out


def _rope_tables(seq_len):
    inv = 1.0 / (ROPE_THETA ** (jnp.arange(0, ROPE_DIM, 2, dtype=jnp.float32) / ROPE_DIM))
    ang = jnp.arange(seq_len, dtype=jnp.float32)[:, None] * inv[None, :]
    return jnp.cos(ang), jnp.sin(ang)


def _partial_rope(x, cos, sin):
    half = ROPE_DIM // 2
    xr = x[..., :ROPE_DIM].astype(jnp.float32)
    x1, x2 = xr[..., :half], xr[..., half:]
    c, s = cos[None, :, None, :], sin[None, :, None, :]
    rot = jnp.concatenate([x1 * c - x2 * s, x2 * c + x1 * s], axis=-1).astype(x.dtype)
    return jnp.concatenate([rot, x[..., ROPE_DIM:]], axis=-1)


def _window_attention(h, w_in, sinks, w_out, cos, sin):
    b, seq_len, _ = h.shape
    kv, grp, dh, blk = ATTN_KV_HEADS, ATTN_GROUP, ATTN_HEAD_DIM, ATTN_BLOCK
    nb = seq_len // blk
    qkv = h @ w_in
    nq = ATTN_Q_HEADS * dh
    q = qkv[..., :nq].reshape(b, seq_len, ATTN_Q_HEADS, dh)
    k = qkv[..., nq:nq + kv * dh].reshape(b, seq_len, kv, dh)
    v = qkv[..., nq + kv * dh:].reshape(b, seq_len, kv, dh)
    q = _partial_rope(q, cos, sin).reshape(b, nb, blk, kv, grp, dh)
    k = _partial_rope(k, cos, sin)

    def band(t):
        tp = jnp.pad(t, ((0, 0), (blk, blk), (0, 0), (0, 0))).reshape(b, nb + 2, blk, kv, dh)
        return jnp.concatenate([tp[:, :-2], tp[:, 1:-1], tp[:, 2:]], axis=2)

    kb, vb = band(k), band(v)
    scores = jnp.einsum('bnqkgd,bnskd->bnkgqs', q, kb).astype(jnp.float32) * (dh ** -0.5)
    blk_idx = jnp.arange(nb)[:, None, None]
    qpos = blk_idx * blk + jnp.arange(blk)[None, :, None]
    kpos = blk_idx * blk - blk + jnp.arange(3 * blk)[None, None, :]
    valid = (jnp.abs(qpos - kpos) <= WINDOW) & (kpos >= 0) & (kpos < seq_len)
    scores = jnp.where(valid[None, :, None, None], scores, -jnp.inf)
    sink = sinks.astype(jnp.float32).reshape(kv, grp)[None, None, :, :, None, None]
    m = jnp.maximum(jnp.max(scores, axis=-1, keepdims=True), sink)
    pexp = jnp.exp(scores - m)
    probs = pexp / (jnp.sum(pexp, axis=-1, keepdims=True) + jnp.exp(sink - m))
    out = jnp.einsum('bnkgqs,bnskd->bnqkgd', probs.astype(h.dtype), vb)
    return out.reshape(b, seq_len, ATTN_Q_HEADS * dh) @ w_out


def _swiglu(h, w_in, w_out):
    u = h @ w_in
    return (jax.nn.silu(u[..., :D_FF]) * u[..., D_FF:]) @ w_out


def _trunk(x, norm_w, ffn_w_in, ffn_w_out, ssd_w_in, ssd_conv_w, ssd_conv_b, ssd_dt_bias,
           ssd_a_log, ssd_d, ssd_norm_w, ssd_w_out, fnet_w_out, attn_w_in, attn_sinks, attn_w_out):
    cos, sin = _rope_tables(x.shape[1])
    for i in range(DEPTH):
        kind, j = i % N_MIXERS, i // N_MIXERS
        h = _rms_norm(x, norm_w[i, 0])
        if kind == 0:
            m = _ssd_mixer(h, ssd_w_in[j], ssd_conv_w[j], ssd_conv_b[j], ssd_dt_bias[j],
                           ssd_a_log[j], ssd_d[j], ssd_norm_w[j], ssd_w_out[j])
        elif kind == 1:
            m = _fourier_mixer(h, fnet_w_out[j])
        else:
            m = _window_attention(h, attn_w_in[j], attn_sinks[j], attn_w_out[j], cos, sin)
        x = x + _rms_norm(m, norm_w[i, 1])
        h = _rms_norm(x, norm_w[i, 2])
        x = x + _rms_norm(_swiglu(h, ffn_w_in[i], ffn_w_out[i]), norm_w[i, 3])
    return x


def setup_inputs(seed: int = 0) -> dict:
    key = jax.random.key(seed)
    ks = jax.random.split(key, 20)
    f32 = jnp.float32
    nrm = lambda k, shape, scale: jax.random.normal(k, shape, f32) * scale
    dt0 = jnp.exp(jax.random.uniform(ks[8], (N_SSD_LAYERS, 2, SSD_HEADS), f32,
                                     np.log(1e-3), np.log(1e-1)))
    return {
        'x_prompt': nrm(ks[0], (BATCH, SEQ, D_MODEL), 1.0),
        'x_sample': nrm(ks[1], (DEC_BATCH, DEC_SEQ, D_MODEL), 1.0),
        'norm_w': 1.0 + nrm(ks[2], (DEPTH, 4, D_MODEL), 0.02),
        'ffn_w_in': nrm(ks[3], (DEPTH, D_MODEL, 2 * D_FF), D_MODEL ** -0.5),
        'ffn_w_out': nrm(ks[4], (DEPTH, D_FF, D_MODEL), D_FF ** -0.5),
        'ssd_w_in': nrm(ks[5], (N_SSD_LAYERS, D_MODEL, SSD_IN_DIM), D_MODEL ** -0.5),
        'ssd_conv_w': nrm(ks[6], (N_SSD_LAYERS, SSD_CONV, CONV_DIM), SSD_CONV ** -0.5),
        'ssd_conv_b': nrm(ks[7], (N_SSD_LAYERS, CONV_DIM), 0.02),
        'ssd_dt_bias': dt0 + jnp.log(-jnp.expm1(-dt0)),
        'ssd_a_log': jnp.log(jax.random.uniform(ks[9], (N_SSD_LAYERS, 2, SSD_HEADS), f32, 1.0, 16.0)),
        'ssd_d': 1.0 + nrm(ks[10], (N_SSD_LAYERS, SSD_HEADS), 0.1),
        'ssd_norm_w': 1.0 + nrm(ks[11], (N_SSD_LAYERS, D_INNER), 0.02),
        'ssd_w_out': nrm(ks[12], (N_SSD_LAYERS, D_INNER, D_MODEL), D_INNER ** -0.5),
        'fnet_w_out': nrm(ks[13], (N_FNET_LAYERS, D_MODEL, D_MODEL), D_MODEL ** -0.5),
        'attn_w_in': nrm(ks[14], (N_ATTN_LAYERS, D_MODEL, ATTN_IN_DIM), D_MODEL ** -0.5),
        'attn_sinks': nrm(ks[15], (N_ATTN_LAYERS, ATTN_Q_HEADS), 0.5),
        'attn_w_out': nrm(ks[16], (N_ATTN_LAYERS, ATTN_Q_HEADS * ATTN_HEAD_DIM, D_MODEL),
                          (ATTN_Q_HEADS * ATTN_HEAD_DIM) ** -0.5),
    }


def reference(x_prompt, x_sample, norm_w, ffn_w_in, ffn_w_out, ssd_w_in, ssd_conv_w, ssd_conv_b,
              ssd_dt_bias, ssd_a_log, ssd_d, ssd_norm_w, ssd_w_out, fnet_w_out, attn_w_in,
              attn_sinks, attn_w_out):
    y_prompt = _trunk(x_prompt, norm_w, ffn_w_in, ffn_w_out, ssd_w_in, ssd_conv_w, ssd_conv_b,
                      ssd_dt_bias, ssd_a_log, ssd_d, ssd_norm_w, ssd_w_out, fnet_w_out,
                      attn_w_in, attn_sinks, attn_w_out)
    y_sample = _trunk(x_sample, norm_w, ffn_w_in, ffn_w_out, ssd_w_in, ssd_conv_w, ssd_conv_b,
                      ssd_dt_bias, ssd_a_log, ssd_d, ssd_norm_w, ssd_w_out, fnet_w_out,
                      attn_w_in, attn_sinks, attn_w_out)
    return (y_prompt, y_sample)
```

```python
import functools

import numpy as np
import jax
import jax.numpy as jnp
from jax import lax
from jax.experimental import pallas as pl
from jax.experimental.pallas import tpu as pltpu

F32 = jnp.float32
BF16 = jnp.bfloat16

D_MODEL = 2048
SEQ = 2048
DEPTH = 4
N_MIXERS = 3
NORM_EPS = 1e-6

D_INNER = 2 * D_MODEL
SSD_HEAD_DIM = 64
SSD_HEADS = D_INNER // SSD_HEAD_DIM
SSD_GROUPS = 8
SSD_HEADS_PER_GROUP = SSD_HEADS // SSD_GROUPS
SSD_STATE = 128
SSD_CONV = 5
SSD_CHUNK = 128
SSD_GROUP_WIDTH = D_INNER // SSD_GROUPS
CONV_DIM = D_INNER + 2 * SSD_GROUPS * SSD_STATE
SSD_MAIN_DIM = D_INNER + CONV_DIM

FNET_GROUPS = 4
FNET_WIDTH = D_MODEL // FNET_GROUPS

ATTN_HEAD_DIM = 64
ATTN_Q_HEADS = D_MODEL // ATTN_HEAD_DIM
ATTN_KV_HEADS = 8
ATTN_GROUP = ATTN_Q_HEADS // ATTN_KV_HEADS
WINDOW = 128
ROPE_THETA = 500000.0
ROPE_DIM = ATTN_HEAD_DIM // 4
ATTN_IN_DIM = (ATTN_Q_HEADS + 2 * ATTN_KV_HEADS) * ATTN_HEAD_DIM

D_FF = ((8 * D_MODEL + 3 * 256 - 1) // (3 * 256)) * 256

VMEM_LIMIT_BYTES = 56 * 1024 * 1024


def _params(*semantics):
    return pltpu.CompilerParams(dimension_semantics=semantics, vmem_limit_bytes=VMEM_LIMIT_BYTES)


def _rms_scale(v, g):
    ms = jnp.mean(v * v, axis=-1, keepdims=True)
    return v * lax.rsqrt(ms + NORM_EPS) * g


def _norm_matmul_kernel(x_ref, g_ref, w_ref, o_ref, h_ref):
    @pl.when(pl.program_id(1) == 0)
    def _():
        h_ref[...] = _rms_scale(x_ref[...], g_ref[...]).astype(BF16)

    o_ref[...] = jnp.dot(h_ref[...], w_ref[...], preferred_element_type=F32).astype(o_ref.dtype)


def norm_matmul(x, g, w, *, tm, tn, out_dtype):
    t, k = x.shape
    n = w.shape[1]
    return pl.pallas_call(
        _norm_matmul_kernel,
        grid=(t // tm, n // tn),
        in_specs=[
            pl.BlockSpec((tm, k), lambda i, j: (i, 0)),
            pl.BlockSpec((1, k), lambda i, j: (0, 0)),
            pl.BlockSpec((k, tn), lambda i, j: (0, j)),
        ],
        out_specs=pl.BlockSpec((tm, tn), lambda i, j: (i, j)),
        out_shape=jax.ShapeDtypeStruct((t, n), out_dtype),
        scratch_shapes=[pltpu.VMEM((tm, k), BF16)],
        compiler_params=_params("parallel", "arbitrary"),
        name="norm_matmul",
    )(x, g, w)


def _matmul_norm_res_kernel(a_ref, w_ref, g_ref, x_ref, o_ref, acc_ref, *, nk):
    k = pl.program_id(1)
    part = jnp.dot(a_ref[...], w_ref[...], preferred_element_type=F32)

    @pl.when(k == 0)
    def _():
        acc_ref[...] = part

    @pl.when(k > 0)
    def _():
        acc_ref[...] += part

    @pl.when(k == nk - 1)
    def _():
        o_ref[...] = x_ref[...] + _rms_scale(acc_ref[...], g_ref[...])


def matmul_norm_res(a, w, g, x, *, tm, tk):
    t, kdim = a.shape
    d = w.shape[1]
    nk = kdim // tk
    return pl.pallas_call(
        functools.partial(_matmul_norm_res_kernel, nk=nk),
        grid=(t // tm, nk),
        in_specs=[
            pl.BlockSpec((tm, tk), lambda i, k: (i, k)),
            pl.BlockSpec((tk, d), lambda i, k: (k, 0)),
            pl.BlockSpec((1, d), lambda i, k: (0, 0)),
            pl.BlockSpec((tm, d), lambda i, k: (i, 0)),
        ],
        out_specs=pl.BlockSpec((tm, d), lambda i, k: (i, 0)),
        out_shape=jax.ShapeDtypeStruct((t, d), F32),
        scratch_shapes=[pltpu.VMEM((tm, d), F32)],
        compiler_params=_params("parallel", "arbitrary"),
        name="matmul_norm_res",
    )(a, w, g, x)


def _ffn_kernel(x_ref, g_in_ref, w_gate_ref, w_up_ref, w_out_ref, g_out_ref, o_ref, h_ref, acc_ref, *, nf):
    f = pl.program_id(1)

    @pl.when(f == 0)
    def _():
        h_ref[...] = _rms_scale(x_ref[...], g_in_ref[...]).astype(BF16)

    h = h_ref[...]
    gate = jnp.dot(h, w_gate_ref[...], preferred_element_type=F32)
    up = jnp.dot(h, w_up_ref[...], preferred_element_type=F32)
    act = (gate * jax.nn.sigmoid(gate) * up).astype(BF16)
    part = jnp.dot(act, w_out_ref[...], preferred_element_type=F32)

    @pl.when(f == 0)
    def _():
        acc_ref[...] = part

    @pl.when(f > 0)
    def _():
        acc_ref[...] += part

    @pl.when(f == nf - 1)
    def _():
        o_ref[...] = x_ref[...] + _rms_scale(acc_ref[...], g_out_ref[...])


def ffn(x, g_in, w_in, w_out, g_out, *, tm, tf):
    t, d = x.shape
    nf = D_FF // tf
    return pl.pallas_call(
        functools.partial(_ffn_kernel, nf=nf),
        grid=(t // tm, nf),
        in_specs=[
            pl.BlockSpec((tm, d), lambda i, f: (i, 0)),
            pl.BlockSpec((1, d), lambda i, f: (0, 0)),
            pl.BlockSpec((d, tf), lambda i, f: (0, f)),
            pl.BlockSpec((d, tf), lambda i, f: (0, nf + f)),
            pl.BlockSpec((tf, d), lambda i, f: (f, 0)),
            pl.BlockSpec((1, d), lambda i, f: (0, 0)),
        ],
        out_specs=pl.BlockSpec((tm, d), lambda i, f: (i, 0)),
        out_shape=jax.ShapeDtypeStruct((t, d), F32),
        scratch_shapes=[pltpu.VMEM((tm, d), BF16), pltpu.VMEM((tm, d), F32)],
        compiler_params=_params("parallel", "arbitrary"),
        name="ffn",
    )(x, g_in, w_in, w_in, w_out, g_out)


SSD_CONV_ROWS = 256
SSD_CONV_HALO = 16
SSD_DT_LANES = 2 * SSD_HEADS_PER_GROUP
SSD_PAIR = 2 * SSD_HEAD_DIM


def _conv_silu(src_ref, w_ref, b_ref, dst_ref):
    rows, halo = SSD_CONV_ROWS, SSD_CONV_HALO
    n_steps = SEQ // rows
    ext_rows = rows + 2 * halo
    pad = SSD_CONV // 2
    w = w_ref[...]
    bias = b_ref[...]

    def body(i, carry):
        r0 = pl.multiple_of(i * rows, rows)
        cur = src_ref[pl.ds(r0, rows), :].astype(F32)
        rp = pl.multiple_of(jnp.maximum(r0 - halo, 0), halo)
        rn = pl.multiple_of(jnp.minimum(r0 + rows, SEQ - halo), halo)
        prev = src_ref[pl.ds(rp, halo), :].astype(F32) * (i > 0).astype(F32)
        nxt = src_ref[pl.ds(rn, halo), :].astype(F32) * (i < n_steps - 1).astype(F32)
        ext = jnp.concatenate([prev, cur, nxt], axis=0)
        acc = cur * w[pad:pad + 1, :] + bias
        for d in range(-pad, pad + 1):
            if d == 0:
                continue
            shifted = pltpu.roll(ext, (ext_rows - d) % ext_rows, 0)[halo:halo + rows]
            acc = acc + shifted * w[pad + d:pad + d + 1, :]
        dst_ref[pl.ds(r0, rows), :] = (acc * jax.nn.sigmoid(acc)).astype(dst_ref.dtype)
        return carry

    lax.fori_loop(0, n_steps, body, 0)


def _prefix_sum(a, axis):
    n = a.shape[axis]
    idx = lax.broadcasted_iota(jnp.int32, a.shape, axis)
    s = 1
    while s < n:
        a = a + jnp.where(idx >= s, pltpu.roll(a, s, axis), 0.0)
        s *= 2
    return a


def _expand_heads(col, heads):
    q = col.shape[0]
    lo = lax.broadcasted_iota(jnp.int32, (q, SSD_PAIR), 1) < SSD_HEAD_DIM
    tiles = []
    for k in range(0, len(heads), 2):
        a = jnp.broadcast_to(col[:, heads[k]:heads[k] + 1], (q, SSD_PAIR))
        b = jnp.broadcast_to(col[:, heads[k + 1]:heads[k + 1] + 1], (q, SSD_PAIR))
        tiles.append(jnp.where(lo, a, b))
    return jnp.concatenate(tiles, axis=1)


def _ssd_chunk(c, backward, refs):
    (dtc_ref, dtr_ref, biasc_ref, biasr_ref, alogc_ref, alogr_ref, xs_ref, bs_ref, cs_ref, st_ref) = refs
    q = SSD_CHUNK
    r0 = pl.multiple_of(c * q, q)
    hpg = SSD_HEADS_PER_GROUP
    heads = [hpg * int(backward) + r for r in range(hpg)]

    xs = xs_ref[pl.ds(r0, q), :]
    bc = bs_ref[pl.ds(r0, q), :]
    cc = cs_ref[pl.ds(r0, q), :]

    dtc = jax.nn.softplus(dtc_ref[pl.ds(r0, q), :] + biasc_ref[...])
    adtc = dtc * (-jnp.exp(alogc_ref[...]))
    pc = _prefix_sum(adtc, 0)
    totc = pc[q - 1:q, :]
    dtr = jax.nn.softplus(dtr_ref[c] + biasr_ref[...])
    adtr = dtr * (-jnp.exp(alogr_ref[...]))
    pr = _prefix_sum(adtr, 1)
    totr = pr[:, q - 1:q]
    if backward:
        uc = totc - pc + adtc
        ur = totr - pr + adtr
    else:
        uc = pc
        ur = pr

    dt_x = _expand_heads(dtc, heads)
    ds_x = _expand_heads(dtc * jnp.exp(totc - uc), heads)
    od_x = _expand_heads(jnp.exp(uc), heads)
    tot_x = _expand_heads(jnp.exp(totc), heads)

    xdt = (xs * dt_x).astype(BF16)
    xds = (xs * ds_x).astype(BF16)

    cb = lax.dot_general(cc, bc, (((1,), (1,)), ((), ())), preferred_element_type=F32)
    row_i = lax.broadcasted_iota(jnp.int32, (q, q), 0)
    col_j = lax.broadcasted_iota(jnp.int32, (q, q), 1)
    mask = (col_j >= row_i) if backward else (col_j <= row_i)
    lane_lo = lax.broadcasted_iota(jnp.int32, (q, SSD_PAIR), 1) < SSD_HEAD_DIM
    zero = jnp.zeros((q, SSD_PAIR), BF16)

    y_tiles = []
    for k in range(hpg // 2):
        ms = []
        for h in (heads[2 * k], heads[2 * k + 1]):
            diff = uc[:, h:h + 1] - ur[h:h + 1, :]
            ms.append((cb * jnp.exp(jnp.where(mask, diff, -jnp.inf))).astype(BF16))
        x2 = xdt[:, k * SSD_PAIR:(k + 1) * SSD_PAIR]
        lhs = jnp.concatenate(ms, axis=1)
        rhs = jnp.concatenate([jnp.where(lane_lo, x2, zero), jnp.where(lane_lo, zero, x2)], axis=0)
        y_tiles.append(jnp.dot(lhs, rhs, preferred_element_type=F32))
    y = jnp.concatenate(y_tiles, axis=1)

    state = st_ref[...]
    y = y + jnp.dot(cc, state.astype(BF16), preferred_element_type=F32) * od_x
    upd = lax.dot_general(bc, xds, (((0,), (0,)), ((), ())), preferred_element_type=F32)
    st_ref[...] = state * tot_x + upd
    return y


def _ssd_kernel(z_ref, x_ref, b_ref, c_ref, dtc_ref, dtr_ref,
                cwx_ref, cwb_ref, cwc_ref, cbx_ref, cbb_ref, cbc_ref,
                biasc_ref, biasr_ref, alogc_ref, alogr_ref, dskip_ref, nw_ref,
                o_ref, xs_ref, bs_ref, cs_ref, y_ref, st_ref):
    _conv_silu(x_ref, cwx_ref, cbx_ref, xs_ref)
    _conv_silu(b_ref, cwb_ref, cbb_ref, bs_ref)
    _conv_silu(c_ref, cwc_ref, cbc_ref, cs_ref)
    refs = (dtc_ref, dtr_ref, biasc_ref, biasr_ref, alogc_ref, alogr_ref, xs_ref, bs_ref, cs_ref, st_ref)
    n_chunks = SEQ // SSD_CHUNK
    q = SSD_CHUNK

    st_ref[...] = jnp.zeros_like(st_ref)

    def fwd_body(c, carry):
        r0 = pl.multiple_of(c * q, q)
        y_ref[pl.ds(r0, q), :] = _ssd_chunk(c, False, refs)
        return carry

    lax.fori_loop(0, n_chunks, fwd_body, 0)

    st_ref[...] = jnp.zeros_like(st_ref)

    def bwd_body(t, carry):
        c = n_chunks - 1 - t
        r0 = pl.multiple_of(c * q, q)
        y = _ssd_chunk(c, True, refs)
        y = y + y_ref[pl.ds(r0, q), :] + dskip_ref[...] * xs_ref[pl.ds(r0, q), :]
        z = z_ref[pl.ds(r0, q), :].astype(F32)
        y = y * (z * jax.nn.sigmoid(z))
        o_ref[pl.ds(r0, q), :] = _rms_scale(y, nw_ref[...]).astype(o_ref.dtype)
        return carry

    lax.fori_loop(0, n_chunks, bwd_body, 0)


def ssd_scan(proj, dt_col, dt_row, conv_w, conv_b, bias_col, bias_row, alog_col, alog_row, d_skip, norm_w):
    t = proj.shape[0]
    nb = t // SEQ
    gw, ns = SSD_GROUP_WIDTH, SSD_STATE
    g_blocks = D_INNER // gw
    b_col0 = (2 * D_INNER) // ns
    c_col0 = b_col0 + SSD_GROUPS
    n_chunks = SEQ // SSD_CHUNK

    def cw(width, col0):
        return pl.BlockSpec((SSD_CONV, width), lambda b, g: (0, col0 + g))

    def cb(width, col0):
        return pl.BlockSpec((1, width), lambda b, g: (0, col0 + g))

    return pl.pallas_call(
        _ssd_kernel,
        grid=(nb, SSD_GROUPS),
        in_specs=[
            pl.BlockSpec((SEQ, gw), lambda b, g: (b, g)),
            pl.BlockSpec((SEQ, gw), lambda b, g: (b, g_blocks + g)),
            pl.BlockSpec((SEQ, ns), lambda b, g: (b, b_col0 + g)),
            pl.BlockSpec((SEQ, ns), lambda b, g: (b, c_col0 + g)),
            pl.BlockSpec((None, SEQ, SSD_DT_LANES), lambda b, g: (g, b, 0)),
            pl.BlockSpec((None, None, n_chunks, SSD_DT_LANES, SSD_CHUNK), lambda b, g: (b, g, 0, 0, 0)),
            cw(gw, 0), cw(ns, D_INNER // ns), cw(ns, D_INNER // ns + SSD_GROUPS),
            cb(gw, 0), cb(ns, D_INNER // ns), cb(ns, D_INNER // ns + SSD_GROUPS),
            pl.BlockSpec((None, 1, SSD_DT_LANES), lambda b, g: (g, 0, 0)),
            pl.BlockSpec((None, SSD_DT_LANES, 1), lambda b, g: (g, 0, 0)),
            pl.BlockSpec((None, 1, SSD_DT_LANES), lambda b, g: (g, 0, 0)),
            pl.BlockSpec((None, SSD_DT_LANES, 1), lambda b, g: (g, 0, 0)),
            pl.BlockSpec((1, gw), lambda b, g: (0, g)),
            pl.BlockSpec((1, gw), lambda b, g: (0, g)),
        ],
        out_specs=pl.BlockSpec((SEQ, gw), lambda b, g: (b, g)),
        out_shape=jax.ShapeDtypeStruct((t, D_INNER), BF16),
        scratch_shapes=[
            pltpu.VMEM((SEQ, gw), F32),
            pltpu.VMEM((SEQ, ns), BF16),
            pltpu.VMEM((SEQ, ns), BF16),
            pltpu.VMEM((SEQ, gw), F32),
            pltpu.VMEM((ns, gw), F32),
        ],
        compiler_params=_params("parallel", "parallel"),
        name="ssd_scan",
    )(proj, proj, proj, proj, dt_col, dt_row, conv_w, conv_w, conv_w, conv_b, conv_b, conv_b,
      bias_col, bias_row, alog_col, alog_row, d_skip, norm_w)


def _ssd_small_params(dt_bias, a_log, d_skip):
    def per_group(p):
        pg = p.reshape(2, SSD_GROUPS, SSD_HEADS_PER_GROUP).transpose(1, 0, 2).reshape(SSD_GROUPS, SSD_DT_LANES)
        return pg[:, None, :], pg[:, :, None]

    bias_col, bias_row = per_group(dt_bias)
    alog_col, alog_row = per_group(a_log)
    return bias_col, bias_row, alog_col, alog_row, jnp.repeat(d_skip, SSD_HEAD_DIM)[None, :]


def _ssd_dt_layouts(dt_raw):
    t = dt_raw.shape[0]
    nb = t // SEQ
    n_chunks = SEQ // SSD_CHUNK
    d = dt_raw.reshape(t, 2, SSD_GROUPS, SSD_HEADS_PER_GROUP)
    dt_col = d.transpose(2, 0, 1, 3).reshape(SSD_GROUPS, t, SSD_DT_LANES)
    d5 = d.reshape(nb, n_chunks, SSD_CHUNK, 2, SSD_GROUPS, SSD_HEADS_PER_GROUP)
    dt_row = d5.transpose(0, 4, 1, 3, 5, 2).reshape(nb, SSD_GROUPS, n_chunks, SSD_DT_LANES, SSD_CHUNK)
    return dt_col, dt_row


def _fnet_chan_kernel(x_ref, g_ref, w_ref, o_ref, h_ref):
    j = pl.program_id(1)

    @pl.when(j == 0)
    def _():
        h = _rms_scale(x_ref[...], g_ref[...])
        for gi in range(FNET_GROUPS):
            h_ref[gi] = h[:, gi * FNET_WIDTH:(gi + 1) * FNET_WIDTH].astype(BF16)

    o_ref[...] = jnp.dot(h_ref[j % FNET_GROUPS], w_ref[...], preferred_element_type=F32).astype(o_ref.dtype)


def fnet_chan(x, g, w_cs, *, tm):
    t, d = x.shape
    return pl.pallas_call(
        _fnet_chan_kernel,
        grid=(t // tm, 2 * FNET_GROUPS),
        in_specs=[
            pl.BlockSpec((tm, d), lambda i, j: (i, 0)),
            pl.BlockSpec((1, d), lambda i, j: (0, 0)),
            pl.BlockSpec((None, FNET_WIDTH, FNET_WIDTH), lambda i, j: (j // FNET_GROUPS, 0, 0)),
        ],
        out_specs=pl.BlockSpec((tm, FNET_WIDTH), lambda i, j: (i, j)),
        out_shape=jax.ShapeDtypeStruct((t, 2 * d), BF16),
        scratch_shapes=[pltpu.VMEM((FNET_GROUPS, tm, FNET_WIDTH), BF16)],
        compiler_params=_params("parallel", "arbitrary"),
        name="fnet_chan",
    )(x, g, w_cs)


def _fnet_seq_kernel(cl_ref, sl_ref, p_ref, q_ref, o_ref):
    acc = jnp.dot(cl_ref[...], p_ref[...], preferred_element_type=F32)
    acc += jnp.dot(sl_ref[...], q_ref[...], preferred_element_type=F32)
    o_ref[...] = acc.astype(o_ref.dtype)


def fnet_seq(cl, sl_neg, pq, *, tm, tn):
    t = pq.shape[0]
    d = pq.shape[1] // 2
    nb = t // SEQ
    ni = SEQ // tm
    nj = d // tn
    return pl.pallas_call(
        _fnet_seq_kernel,
        grid=(nb, nj, ni),
        in_specs=[
            pl.BlockSpec((tm, SEQ), lambda b, j, i: (i, 0)),
            pl.BlockSpec((tm, SEQ), lambda b, j, i: (i, 0)),
            pl.BlockSpec((SEQ, tn), lambda b, j, i: (b, j)),
            pl.BlockSpec((SEQ, tn), lambda b, j, i: (b, nj + j)),
        ],
        out_specs=pl.BlockSpec((tm, tn), lambda b, j, i: (b * ni + i, j)),
        out_shape=jax.ShapeDtypeStruct((t, d), BF16),
        compiler_params=_params("parallel", "parallel", "arbitrary"),
        name="fnet_seq",
    )(cl, sl_neg, pq, pq)


def _dft_tables():
    def cos_sin(n):
        idx = jnp.arange(n, dtype=jnp.int32)
        ang = ((idx[:, None] * idx[None, :]) % n).astype(F32) * (2.0 * np.pi / n)
        return jnp.cos(ang), jnp.sin(ang)

    cc, sc = cos_sin(FNET_WIDTH)
    cl, sl = cos_sin(SEQ)
    scale = 1.0 / np.sqrt(SEQ * FNET_WIDTH)
    w_cs = jnp.stack([cc, sc]).astype(BF16)
    return w_cs, (cl * scale).astype(BF16), (-sl * scale).astype(BF16)


ATTN_PAIR = 2 * ATTN_HEAD_DIM
ATTN_KEYS = 3 * WINDOW
ATTN_Q_TILE = 2 * ATTN_GROUP * ATTN_HEAD_DIM


def _rope(v, tc, ta, tb):
    n = v.shape[-1]
    half = ROPE_DIM // 2
    return v * tc + pltpu.roll(v, n - half, 1) * ta + pltpu.roll(v, half, 1) * tb


def _attn_kernel(q_ref, k_ref, v_ref, tcq_ref, taq_ref, tbq_ref, tck_ref, tak_ref, tbk_ref, sink_ref,
                 o_ref, kr_ref):
    n = pl.program_id(2)

    @pl.when(n == 0)
    def _():
        kr_ref[...] = _rope(k_ref[...], tck_ref[...], tak_ref[...], tbk_ref[...]).astype(BF16)

    start = pl.multiple_of(jnp.clip((n - 1) * WINDOW, 0, SEQ - ATTN_KEYS), WINDOW)
    kw = kr_ref[pl.ds(start, ATTN_KEYS), :]
    vw = v_ref[pl.ds(start, ATTN_KEYS), :].astype(BF16)
    lane_k = lax.broadcasted_iota(jnp.int32, (ATTN_KEYS, ATTN_PAIR), 1)
    kw_sw = pltpu.roll(kw.astype(F32), ATTN_HEAD_DIM, 1).astype(BF16)
    vw_sw = pltpu.roll(vw.astype(F32), ATTN_HEAD_DIM, 1).astype(BF16)
    lo = lane_k < ATTN_HEAD_DIM
    zero = jnp.zeros_like(vw)
    k_dup = (jnp.where(lo, kw, kw_sw), jnp.where(lo, kw_sw, kw))
    v_lo = (jnp.where(lo, vw, zero), jnp.where(lo, vw_sw, zero))
    v_hi = (jnp.where(lo, zero, vw_sw), jnp.where(lo, zero, vw))

    qpos = n * WINDOW + lax.broadcasted_iota(jnp.int32, (WINDOW, ATTN_KEYS), 0)
    kpos = start + lax.broadcasted_iota(jnp.int32, (WINDOW, ATTN_KEYS), 1)
    valid = jnp.abs(qpos - kpos) <= WINDOW
    lane_q = lax.broadcasted_iota(jnp.int32, (WINDOW, ATTN_PAIR), 1)
    q_lo = lane_q < ATTN_HEAD_DIM

    for pair in range(ATTN_Q_TILE // ATTN_PAIR):
        kv = pair // (ATTN_GROUP // 2)
        cols = slice(pair * ATTN_PAIR, (pair + 1) * ATTN_PAIR)
        q = _rope(q_ref[:, cols], tcq_ref[...], taq_ref[...], tbq_ref[...]) * (ATTN_HEAD_DIM ** -0.5)
        out = None
        for half in range(2):
            qh = jnp.where(q_lo if half == 0 else ~q_lo, q, 0.0).astype(BF16)
            s = lax.dot_general(qh, k_dup[kv], (((1,), (1,)), ((), ())), preferred_element_type=F32)
            s = jnp.where(valid, s, -jnp.inf)
            sink = sink_ref[2 * pair + half:2 * pair + half + 1, :]
            m = jnp.maximum(jnp.max(s, axis=-1, keepdims=True), sink)
            p = jnp.exp(s - m)
            denom = jnp.sum(p, axis=-1, keepdims=True) + jnp.exp(sink - m)
            vsel = v_lo[kv] if half == 0 else v_hi[kv]
            pv = jnp.dot(p.astype(BF16), vsel, preferred_element_type=F32) / denom
            out = pv if out is None else out + pv
        o_ref[:, cols] = out.astype(o_ref.dtype)


def window_attention(qkv, tables, sinks):
    t = qkv.shape[0]
    nb = t // SEQ
    nblk = SEQ // WINDOW
    n_pairs = ATTN_KV_HEADS // 2
    k_col0 = ATTN_Q_HEADS * ATTN_HEAD_DIM // ATTN_PAIR
    v_col0 = k_col0 + n_pairs
    tc, ta, tb = tables
    q_tab = pl.BlockSpec((WINDOW, ATTN_PAIR), lambda b, kp, n: (n, 0))
    k_tab = pl.BlockSpec((SEQ, ATTN_PAIR), lambda b, kp, n: (0, 0))
    return pl.pallas_call(
        _attn_kernel,
        grid=(nb, n_pairs, nblk),
        in_specs=[
            pl.BlockSpec((WINDOW, ATTN_Q_TILE), lambda b, kp, n: (b * nblk + n, kp)),
            pl.BlockSpec((SEQ, ATTN_PAIR), lambda b, kp, n: (b, k_col0 + kp)),
            pl.BlockSpec((SEQ, ATTN_PAIR), lambda b, kp, n: (b, v_col0 + kp)),
            q_tab, q_tab, q_tab, k_tab, k_tab, k_tab,
            pl.BlockSpec((None, 2 * ATTN_GROUP, 1), lambda b, kp, n: (kp, 0, 0)),
        ],
        out_specs=pl.BlockSpec((WINDOW, ATTN_Q_TILE), lambda b, kp, n: (b * nblk + n, kp)),
        out_shape=jax.ShapeDtypeStruct((t, ATTN_Q_HEADS * ATTN_HEAD_DIM), BF16),
        scratch_shapes=[pltpu.VMEM((SEQ, ATTN_PAIR), BF16)],
        compiler_params=_params("parallel", "parallel", "arbitrary"),
        name="window_attention",
    )(qkv, qkv, qkv, tc, ta, tb, tc, ta, tb, sinks.reshape(n_pairs, 2 * ATTN_GROUP, 1))


def _rope_lane_tables():
    half = ROPE_DIM // 2
    inv = 1.0 / (ROPE_THETA ** (jnp.arange(0, ROPE_DIM, 2, dtype=F32) / ROPE_DIM))
    ang = jnp.arange(SEQ, dtype=F32)[:, None] * inv[None, :]
    cos, sin = jnp.cos(ang), jnp.sin(ang)
    pad = ATTN_HEAD_DIM - ROPE_DIM
    tc = jnp.concatenate([cos, cos, jnp.ones((SEQ, pad), F32)], axis=1)
    ta = jnp.concatenate([-sin, jnp.zeros((SEQ, half + pad), F32)], axis=1)
    tb = jnp.concatenate([jnp.zeros((SEQ, half), F32), sin, jnp.zeros((SEQ, pad), F32)], axis=1)
    return tuple(jnp.tile(tab, (1, 2)) for tab in (tc, ta, tb))


def _ssd_layer(x, g_pre, g_post, w_in, conv_w, conv_b, dt_bias, a_log, d_skip, norm_w, w_out):
    w_main = w_in[:, :SSD_MAIN_DIM].astype(BF16)
    w_dt = w_in[:, SSD_MAIN_DIM:].astype(BF16)
    proj = norm_matmul(x, g_pre, w_main, tm=1024, tn=1024, out_dtype=BF16)
    dt_raw = norm_matmul(x, g_pre, w_dt, tm=1024, tn=w_dt.shape[1], out_dtype=F32)
    dt_col, dt_row = _ssd_dt_layouts(dt_raw)
    small = _ssd_small_params(dt_bias, a_log, d_skip)
    y = ssd_scan(proj, dt_col, dt_row, conv_w, conv_b[None, :], *small, norm_w[None, :])
    return matmul_norm_res(y, w_out.astype(BF16), g_post, x, tm=512, tk=1024)


def _fnet_layer(x, g_pre, g_post, w_out, dft):
    w_cs, cl, sl_neg = dft
    pq = fnet_chan(x, g_pre, w_cs, tm=1024)
    f = fnet_seq(cl, sl_neg, pq, tm=1024, tn=1024)
    return matmul_norm_res(f, w_out.astype(BF16), g_post, x, tm=512, tk=D_MODEL)


def _attn_layer(x, g_pre, g_post, w_in, sinks, w_out, rope):
    qkv = norm_matmul(x, g_pre, w_in.astype(BF16), tm=1024, tn=1024, out_dtype=F32)
    a = window_attention(qkv, rope, sinks)
    return matmul_norm_res(a, w_out.astype(BF16), g_post, x, tm=512, tk=D_MODEL)


def kernel(x_prompt, x_sample, norm_w, ffn_w_in, ffn_w_out, ssd_w_in, ssd_conv_w, ssd_conv_b, ssd_dt_bias,
           ssd_a_log, ssd_d, ssd_norm_w, ssd_w_out, fnet_w_out, attn_w_in, attn_sinks, attn_w_out):
    n_prompt = x_prompt.shape[0]
    assert x_prompt.shape[1:] == (SEQ, D_MODEL) and x_sample.shape[1:] == (SEQ, D_MODEL)
    x = jnp.concatenate([x_prompt, x_sample], axis=0).reshape(-1, D_MODEL)
    dft = _dft_tables()
    rope = _rope_lane_tables()
    for i in range(DEPTH):
        kind, j = i % N_MIXERS, i // N_MIXERS
        g = norm_w[i][:, None, :]
        if kind == 0:
            x = _ssd_layer(x, g[0], g[1], ssd_w_in[j], ssd_conv_w[j], ssd_conv_b[j], ssd_dt_bias[j],
                           ssd_a_log[j], ssd_d[j], ssd_norm_w[j], ssd_w_out[j])
        elif kind == 1:
            x = _fnet_layer(x, g[0], g[1], fnet_w_out[j], dft)
        else:
            x = _attn_layer(x, g[0], g[1], attn_w_in[j], attn_sinks[j], attn_w_out[j], rope)
        x = ffn(x, g[2], ffn_w_in[i].astype(BF16), ffn_w_out[i].astype(BF16), g[3], tm=512, tf=512)
    x = x.reshape(-1, SEQ, D_MODEL)
    return (x[:n_prompt], x[n_prompt:])
```

```python
import functools

import numpy as np
import jax
import jax.numpy as jnp
from jax import lax
from jax.experimental import pallas as pl
from jax.experimental.pallas import tpu as pltpu

F32 = jnp.float32
BF16 = jnp.bfloat16

D_MODEL = 2048
SEQ = 2048
DEPTH = 4
N_MIXERS = 3
NORM_EPS = 1e-6

D_INNER = 2 * D_MODEL
SSD_HEAD_DIM = 64
SSD_HEADS = D_INNER // SSD_HEAD_DIM
SSD_GROUPS = 8
SSD_HEADS_PER_GROUP = SSD_HEADS // SSD_GROUPS
SSD_STATE = 128
SSD_CONV = 5
SSD_CHUNK = 128
SSD_GROUP_WIDTH = D_INNER // SSD_GROUPS
CONV_DIM = D_INNER + 2 * SSD_GROUPS * SSD_STATE
SSD_MAIN_DIM = D_INNER + CONV_DIM

FNET_GROUPS = 4
FNET_WIDTH = D_MODEL // FNET_GROUPS

ATTN_HEAD_DIM = 64
ATTN_Q_HEADS = D_MODEL // ATTN_HEAD_DIM
ATTN_KV_HEADS = 8
ATTN_GROUP = ATTN_Q_HEADS // ATTN_KV_HEADS
WINDOW = 128
ROPE_THETA = 500000.0
ROPE_DIM = ATTN_HEAD_DIM // 4
ATTN_IN_DIM = (ATTN_Q_HEADS + 2 * ATTN_KV_HEADS) * ATTN_HEAD_DIM

D_FF = ((8 * D_MODEL + 3 * 256 - 1) // (3 * 256)) * 256

VMEM_LIMIT_BYTES = 56 * 1024 * 1024


def _params(*semantics):
    return pltpu.CompilerParams(dimension_semantics=semantics, vmem_limit_bytes=VMEM_LIMIT_BYTES)


def _rms_scale(v, g):
    ms = jnp.mean(v * v, axis=-1, keepdims=True)
    return v * lax.rsqrt(ms + NORM_EPS) * g


def _norm_matmul_kernel(x_ref, g_ref, w_ref, o_ref, h_ref):
    @pl.when(pl.program_id(1) == 0)
    def _():
        h_ref[...] = _rms_scale(x_ref[...], g_ref[...]).astype(BF16)

    o_ref[...] = jnp.dot(h_ref[...], w_ref[...], preferred_element_type=F32).astype(o_ref.dtype)


def norm_matmul(x, g, w, *, tm, tn, out_dtype):
    t, k = x.shape
    n = w.shape[1]
    return pl.pallas_call(
        _norm_matmul_kernel,
        grid=(t // tm, n // tn),
        in_specs=[
            pl.BlockSpec((tm, k), lambda i, j: (i, 0)),
            pl.BlockSpec((1, k), lambda i, j: (0, 0)),
            pl.BlockSpec((k, tn), lambda i, j: (0, j)),
        ],
        out_specs=pl.BlockSpec((tm, tn), lambda i, j: (i, j)),
        out_shape=jax.ShapeDtypeStruct((t, n), out_dtype),
        scratch_shapes=[pltpu.VMEM((tm, k), BF16)],
        compiler_params=_params("parallel", "arbitrary"),
        name="norm_matmul",
    )(x, g, w)


def _matmul_norm_res_kernel(a_ref, w_ref, g_ref, x_ref, o_ref, *, nk):
    if nk == 1:
        m = jnp.dot(a_ref[...], w_ref[...], preferred_element_type=F32)
        o_ref[...] = x_ref[...] + _rms_scale(m, g_ref[...])
        return
    k = pl.program_id(1)

    @pl.when(k == 0)
    def _():
        o_ref[...] = jnp.zeros_like(o_ref)

    o_ref[...] += jnp.dot(a_ref[...], w_ref[...], preferred_element_type=F32)

    @pl.when(k == nk - 1)
    def _():
        o_ref[...] = x_ref[...] + _rms_scale(o_ref[...], g_ref[...])


def matmul_norm_res(a, w, g, x, *, tm, tk):
    t, kdim = a.shape
    d = w.shape[1]
    nk = kdim // tk
    return pl.pallas_call(
        functools.partial(_matmul_norm_res_kernel, nk=nk),
        grid=(t // tm, nk),
        in_specs=[
            pl.BlockSpec((tm, tk), lambda i, k: (i, k)),
            pl.BlockSpec((tk, d), lambda i, k: (k, 0), pipeline_mode=pl.Buffered(1) if nk == 1 else None),
            pl.BlockSpec((1, d), lambda i, k: (0, 0)),
            pl.BlockSpec((tm, d), lambda i, k: (i, 0), pipeline_mode=pl.Buffered(1)),
        ],
        out_specs=pl.BlockSpec((tm, d), lambda i, k: (i, 0)),
        out_shape=jax.ShapeDtypeStruct((t, d), F32),
        compiler_params=_params("parallel", "arbitrary"),
        name="matmul_norm_res",
    )(a, w, g, x)


def _ffn_kernel(x_ref, g_in_ref, w_gate_ref, w_up_ref, w_out_ref, g_out_ref, o_ref, h_ref, *, nf):
    f = pl.program_id(1)

    @pl.when(f == 0)
    def _():
        h_ref[...] = _rms_scale(x_ref[...], g_in_ref[...]).astype(BF16)
        o_ref[...] = jnp.zeros_like(o_ref)

    h = h_ref[...]
    gate = jnp.dot(h, w_gate_ref[...], preferred_element_type=F32)
    up = jnp.dot(h, w_up_ref[...], preferred_element_type=F32)
    act = (gate * jax.nn.sigmoid(gate) * up).astype(BF16)
    o_ref[...] += jnp.dot(act, w_out_ref[...], preferred_element_type=F32)

    @pl.when(f == nf - 1)
    def _():
        o_ref[...] = x_ref[...] + _rms_scale(o_ref[...], g_out_ref[...])


def ffn(x, g_in, w_in, w_out, g_out, *, tm, tf):
    t, d = x.shape
    nf = D_FF // tf
    return pl.pallas_call(
        functools.partial(_ffn_kernel, nf=nf),
        grid=(t // tm, nf),
        in_specs=[
            pl.BlockSpec((tm, d), lambda i, f: (i, 0), pipeline_mode=pl.Buffered(1)),
            pl.BlockSpec((1, d), lambda i, f: (0, 0)),
            pl.BlockSpec((d, tf), lambda i, f: (0, f)),
            pl.BlockSpec((d, tf), lambda i, f: (0, nf + f)),
            pl.BlockSpec((tf, d), lambda i, f: (f, 0)),
            pl.BlockSpec((1, d), lambda i, f: (0, 0)),
        ],
        out_specs=pl.BlockSpec((tm, d), lambda i, f: (i, 0)),
        out_shape=jax.ShapeDtypeStruct((t, d), F32),
        scratch_shapes=[pltpu.VMEM((tm, d), BF16)],
        compiler_params=_params("parallel", "arbitrary"),
        name="ffn",
    )(x, g_in, w_in, w_in, w_out, g_out)


SSD_CONV_ROWS = 256
SSD_CONV_HALO = 16
SSD_DT_LANES = 2 * SSD_HEADS_PER_GROUP
SSD_PAIR = 2 * SSD_HEAD_DIM


def _conv_silu(src_ref, w_ref, b_ref, dst_ref):
    rows, halo = SSD_CONV_ROWS, SSD_CONV_HALO
    n_steps = SEQ // rows
    ext_rows = rows + 2 * halo
    pad = SSD_CONV // 2
    w = w_ref[...]
    bias = b_ref[...]

    def body(i, carry):
        r0 = pl.multiple_of(i * rows, rows)
        cur = src_ref[pl.ds(r0, rows), :].astype(F32)
        rp = pl.multiple_of(jnp.maximum(r0 - halo, 0), halo)
        rn = pl.multiple_of(jnp.minimum(r0 + rows, SEQ - halo), halo)
        prev = src_ref[pl.ds(rp, halo), :].astype(F32) * (i > 0).astype(F32)
        nxt = src_ref[pl.ds(rn, halo), :].astype(F32) * (i < n_steps - 1).astype(F32)
        ext = jnp.concatenate([prev, cur, nxt], axis=0)
        acc = cur * w[pad:pad + 1, :] + bias
        for d in range(-pad, pad + 1):
            if d == 0:
                continue
            shifted = pltpu.roll(ext, (ext_rows - d) % ext_rows, 0)[halo:halo + rows]
            acc = acc + shifted * w[pad + d:pad + d + 1, :]
        dst_ref[pl.ds(r0, rows), :] = (acc * jax.nn.sigmoid(acc)).astype(dst_ref.dtype)
        return carry

    lax.fori_loop(0, n_steps, body, 0)


def _prefix_sum(a, axis):
    n = a.shape[axis]
    idx = lax.broadcasted_iota(jnp.int32, a.shape, axis)
    s = 1
    while s < n:
        a = a + jnp.where(idx >= s, pltpu.roll(a, s, axis), 0.0)
        s *= 2
    return a


LOG2_E = float(np.log2(np.e))


def _ssd_token_tables(dtr_ref, biasr_ref, alogr_ref, bs_ref, w_ref, v_ref, ds_ref, et_ref, wcol_ref, bt_ref):
    q = SSD_CHUNK
    n_chunks = SEQ // q
    rows = n_chunks * SSD_DT_LANES
    dt = jax.nn.softplus(dtr_ref[...].reshape(rows, q) + biasr_ref[...])
    adt = dt * (-jnp.exp(alogr_ref[...]))
    pre = _prefix_sum(adt, 1)
    tot = pre[:, q - 1:q]
    row = lax.broadcasted_iota(jnp.int32, (rows, q), 0)
    backward = (row % SSD_DT_LANES) >= SSD_HEADS_PER_GROUP
    u = jnp.where(backward, tot - pre + adt, pre)
    w = u * LOG2_E
    w_ref[...] = w
    v_ref[...] = w - jnp.log2(dt)
    ds_ref[...] = dt * jnp.exp(tot - u)
    et_ref[...] = jnp.broadcast_to(jnp.exp(tot), (rows, q))

    def body(c, carry):
        r0 = pl.multiple_of(c * SSD_DT_LANES, SSD_DT_LANES)
        tile = jnp.concatenate([w_ref[pl.ds(r0, SSD_DT_LANES), :], jnp.zeros((q - SSD_DT_LANES, q), F32)], axis=0)
        wcol_ref[c] = tile.T
        t0 = pl.multiple_of(c * q, q)
        bt_ref[c] = bs_ref[pl.ds(t0, q), :].astype(F32).T
        return carry

    lax.fori_loop(0, n_chunks, body, 0)


def _ssd_chunk(c, backward, refs, st_ref):
    xs_ref, bs_ref, cs_ref, v_ref, ds_ref, et_ref, wcol_ref, bt_ref = refs
    q = SSD_CHUNK
    hpg = SSD_HEADS_PER_GROUP
    h0 = hpg * int(backward)
    t0 = pl.multiple_of(c * q, q)
    r0 = pl.multiple_of(c * SSD_DT_LANES + h0, hpg)

    v = v_ref[pl.ds(r0, hpg), :]
    dsr = ds_ref[pl.ds(r0, hpg), :]
    etot = et_ref[pl.ds(r0, hpg), :]
    w_col = wcol_ref[c]
    bt = bt_ref[c]

    xs = xs_ref[pl.ds(t0, q), :].astype(BF16)
    bc = bs_ref[pl.ds(t0, q), :]
    cc = cs_ref[pl.ds(t0, q), :]
    cb = lax.dot_general(cc, bc, (((1,), (1,)), ((), ())), preferred_element_type=F32)
    row_i = lax.broadcasted_iota(jnp.int32, (q, q), 0)
    col_j = lax.broadcasted_iota(jnp.int32, (q, q), 1)
    mask = (col_j >= row_i) if backward else (col_j <= row_i)
    lane_lo = lax.broadcasted_iota(jnp.int32, (q, SSD_PAIR), 1) < SSD_HEAD_DIM
    zero = jnp.zeros((q, SSD_PAIR), BF16)

    y_tiles = []
    for k in range(hpg // 2):
        cols = slice(k * SSD_PAIR, (k + 1) * SSD_PAIR)
        ms, bts, ods, ets = [], [], [], []
        for h in (2 * k, 2 * k + 1):
            w_i = jnp.broadcast_to(w_col[:, h0 + h:h0 + h + 1], (q, q))
            decay_dt = jnp.exp2(jnp.where(mask, w_i - v[h:h + 1, :], -jnp.inf))
            ms.append((cb * decay_dt).astype(BF16))
            bts.append((bt * dsr[h:h + 1, :]).astype(BF16))
            ods.append(jnp.exp2(w_i))
            ets.append(jnp.broadcast_to(etot[h:h + 1, :], (SSD_STATE, SSD_PAIR)))
        x2 = xs[:, cols]
        lhs = jnp.concatenate([jnp.concatenate(ms, axis=1), jnp.concatenate(bts, axis=1)], axis=0)
        rhs = jnp.concatenate([jnp.where(lane_lo, x2, zero), jnp.where(lane_lo, zero, x2)], axis=0)
        res = jnp.dot(lhs, rhs, preferred_element_type=F32)
        state = st_ref[:, cols]
        y_off = jnp.dot(cc, state.astype(BF16), preferred_element_type=F32)
        y_tiles.append(res[:q] + y_off * jnp.where(lane_lo, ods[0], ods[1]))
        st_ref[:, cols] = state * jnp.where(lane_lo, ets[0], ets[1]) + res[q:]
    return jnp.concatenate(y_tiles, axis=1)


def _ssd_kernel(z_ref, x_ref, b_ref, c_ref, dtr_ref,
                cwx_ref, cwb_ref, cwc_ref, cbx_ref, cbb_ref, cbc_ref,
                biasr_ref, alogr_ref, dskip_ref, nw_ref,
                o_ref, xs_ref, bs_ref, cs_ref, yf_ref, yb_ref, stf_ref, stb_ref,
                w_ref, v_ref, ds_ref, et_ref, wcol_ref, bt_ref):
    _conv_silu(x_ref, cwx_ref, cbx_ref, xs_ref)
    _conv_silu(b_ref, cwb_ref, cbb_ref, bs_ref)
    _conv_silu(c_ref, cwc_ref, cbc_ref, cs_ref)
    _ssd_token_tables(dtr_ref, biasr_ref, alogr_ref, bs_ref, w_ref, v_ref, ds_ref, et_ref, wcol_ref, bt_ref)
    refs = (xs_ref, bs_ref, cs_ref, v_ref, ds_ref, et_ref, wcol_ref, bt_ref)
    n_chunks = SEQ // SSD_CHUNK
    q = SSD_CHUNK
    stf_ref[...] = jnp.zeros_like(stf_ref)
    stb_ref[...] = jnp.zeros_like(stb_ref)

    def rows(c):
        return pl.ds(pl.multiple_of(c * q, q), q)

    def finish(c, y):
        y = y + dskip_ref[...] * xs_ref[rows(c), :]
        z = z_ref[rows(c), :].astype(F32)
        y = y * (z * jax.nn.sigmoid(z))
        o_ref[rows(c), :] = _rms_scale(y, nw_ref[...]).astype(o_ref.dtype)

    def first_half(t, carry):
        cf, cbk = t, n_chunks - 1 - t
        yf_ref[rows(cf), :] = _ssd_chunk(cf, False, refs, stf_ref)
        yb_ref[rows(cbk), :] = _ssd_chunk(cbk, True, refs, stb_ref)
        return carry

    def second_half(t, carry):
        cf, cbk = t, n_chunks - 1 - t
        finish(cf, _ssd_chunk(cf, False, refs, stf_ref) + yb_ref[rows(cf), :])
        finish(cbk, _ssd_chunk(cbk, True, refs, stb_ref) + yf_ref[rows(cbk), :])
        return carry

    lax.fori_loop(0, n_chunks // 2, first_half, 0, unroll=2)
    lax.fori_loop(n_chunks // 2, n_chunks, second_half, 0, unroll=2)


def ssd_scan(proj, dt_row, conv_w, conv_b, bias_row, alog_row, d_skip, norm_w):
    t = proj.shape[0]
    nb = t // SEQ
    gw, ns = SSD_GROUP_WIDTH, SSD_STATE
    g_blocks = D_INNER // gw
    b_col0 = (2 * D_INNER) // ns
    c_col0 = b_col0 + SSD_GROUPS
    n_chunks = SEQ // SSD_CHUNK
    table_rows = n_chunks * SSD_DT_LANES

    def cw(width, col0):
        return pl.BlockSpec((SSD_CONV, width), lambda b, g: (0, col0 + g))

    def cb(width, col0):
        return pl.BlockSpec((1, width), lambda b, g: (0, col0 + g))

    return pl.pallas_call(
        _ssd_kernel,
        grid=(nb, SSD_GROUPS),
        in_specs=[
            pl.BlockSpec((SEQ, gw), lambda b, g: (b, g)),
            pl.BlockSpec((SEQ, gw), lambda b, g: (b, g_blocks + g)),
            pl.BlockSpec((SEQ, ns), lambda b, g: (b, b_col0 + g)),
            pl.BlockSpec((SEQ, ns), lambda b, g: (b, c_col0 + g)),
            pl.BlockSpec((None, None, n_chunks, SSD_DT_LANES, SSD_CHUNK), lambda b, g: (b, g, 0, 0, 0)),
            cw(gw, 0), cw(ns, D_INNER // ns), cw(ns, D_INNER // ns + SSD_GROUPS),
            cb(gw, 0), cb(ns, D_INNER // ns), cb(ns, D_INNER // ns + SSD_GROUPS),
            pl.BlockSpec((None, table_rows, 1), lambda b, g: (g, 0, 0)),
            pl.BlockSpec((None, table_rows, 1), lambda b, g: (g, 0, 0)),
            pl.BlockSpec((1, gw), lambda b, g: (0, g)),
            pl.BlockSpec((1, gw), lambda b, g: (0, g)),
        ],
        out_specs=pl.BlockSpec((SEQ, gw), lambda b, g: (b, g)),
        out_shape=jax.ShapeDtypeStruct((t, D_INNER), BF16),
        scratch_shapes=[
            pltpu.VMEM((SEQ, gw), F32),
            pltpu.VMEM((SEQ, ns), BF16),
            pltpu.VMEM((SEQ, ns), BF16),
            pltpu.VMEM((SEQ, gw), F32),
            pltpu.VMEM((SEQ, gw), F32),
            pltpu.VMEM((ns, gw), F32),
            pltpu.VMEM((ns, gw), F32),
            pltpu.VMEM((table_rows, SSD_CHUNK), F32),
            pltpu.VMEM((table_rows, SSD_CHUNK), F32),
            pltpu.VMEM((table_rows, SSD_CHUNK), F32),
            pltpu.VMEM((table_rows, SSD_CHUNK), F32),
            pltpu.VMEM((n_chunks, SSD_CHUNK, SSD_CHUNK), F32),
            pltpu.VMEM((n_chunks, ns, SSD_CHUNK), F32),
        ],
        compiler_params=_params("parallel", "parallel"),
        name="ssd_scan",
    )(proj, proj, proj, proj, dt_row, conv_w, conv_w, conv_w, conv_b, conv_b, conv_b,
      bias_row, alog_row, d_skip, norm_w)


def _ssd_small_params(dt_bias, a_log, d_skip):
    def per_group(p):
        pg = p.reshape(2, SSD_GROUPS, SSD_HEADS_PER_GROUP).transpose(1, 0, 2)
        return jnp.tile(pg.reshape(SSD_GROUPS, SSD_DT_LANES, 1), (1, SEQ // SSD_CHUNK, 1))

    return per_group(dt_bias), per_group(a_log), jnp.repeat(d_skip, SSD_HEAD_DIM)[None, :]


def _ssd_dt_layout(dt_raw):
    nb = dt_raw.shape[0] // SEQ
    n_chunks = SEQ // SSD_CHUNK
    d5 = dt_raw.reshape(nb, n_chunks, SSD_CHUNK, 2, SSD_GROUPS, SSD_HEADS_PER_GROUP)
    return d5.transpose(0, 4, 1, 3, 5, 2).reshape(nb, SSD_GROUPS, n_chunks, SSD_DT_LANES, SSD_CHUNK)


def _fnet_chan_kernel(x_ref, g_ref, w_ref, o_ref, h_ref):
    j = pl.program_id(1)

    @pl.when(j == 0)
    def _():
        h = _rms_scale(x_ref[...], g_ref[...])
        for gi in range(FNET_GROUPS):
            h_ref[gi] = h[:, gi * FNET_WIDTH:(gi + 1) * FNET_WIDTH].astype(BF16)

    o_ref[...] = jnp.dot(h_ref[j % FNET_GROUPS], w_ref[...], preferred_element_type=F32).astype(o_ref.dtype)


def fnet_chan(x, g, w_cs, *, tm):
    t, d = x.shape
    return pl.pallas_call(
        _fnet_chan_kernel,
        grid=(t // tm, 2 * FNET_GROUPS),
        in_specs=[
            pl.BlockSpec((tm, d), lambda i, j: (i, 0)),
            pl.BlockSpec((1, d), lambda i, j: (0, 0)),
            pl.BlockSpec((None, FNET_WIDTH, FNET_WIDTH), lambda i, j: (j // FNET_GROUPS, 0, 0)),
        ],
        out_specs=pl.BlockSpec((tm, FNET_WIDTH), lambda i, j: (i, j)),
        out_shape=jax.ShapeDtypeStruct((t, 2 * d), BF16),
        scratch_shapes=[pltpu.VMEM((FNET_GROUPS, tm, FNET_WIDTH), BF16)],
        compiler_params=_params("parallel", "arbitrary"),
        name="fnet_chan",
    )(x, g, w_cs)


def _fnet_seq_kernel(cl_ref, sl_ref, p_ref, q_ref, o_ref):
    acc = jnp.dot(cl_ref[...], p_ref[...], preferred_element_type=F32)
    acc += jnp.dot(sl_ref[...], q_ref[...], preferred_element_type=F32)
    o_ref[...] = acc.astype(o_ref.dtype)


def fnet_seq(cl, sl_neg, pq, *, tm, tn):
    t = pq.shape[0]
    d = pq.shape[1] // 2
    nb = t // SEQ
    ni = SEQ // tm
    nj = d // tn
    return pl.pallas_call(
        _fnet_seq_kernel,
        grid=(nb, nj, ni),
        in_specs=[
            pl.BlockSpec((tm, SEQ), lambda b, j, i: (i, 0)),
            pl.BlockSpec((tm, SEQ), lambda b, j, i: (i, 0)),
            pl.BlockSpec((SEQ, tn), lambda b, j, i: (b, j)),
            pl.BlockSpec((SEQ, tn), lambda b, j, i: (b, nj + j)),
        ],
        out_specs=pl.BlockSpec((tm, tn), lambda b, j, i: (b * ni + i, j)),
        out_shape=jax.ShapeDtypeStruct((t, d), BF16),
        compiler_params=_params("parallel", "parallel", "arbitrary"),
        name="fnet_seq",
    )(cl, sl_neg, pq, pq)


def _dft_tables():
    def cos_sin(n):
        idx = jnp.arange(n, dtype=jnp.int32)
        ang = ((idx[:, None] * idx[None, :]) % n).astype(F32) * (2.0 * np.pi / n)
        return jnp.cos(ang), jnp.sin(ang)

    cc, sc = cos_sin(FNET_WIDTH)
    cl, sl = cos_sin(SEQ)
    scale = 1.0 / np.sqrt(SEQ * FNET_WIDTH)
    w_cs = jnp.stack([cc, sc]).astype(BF16)
    return w_cs, (cl * scale).astype(BF16), (-sl * scale).astype(BF16)


ATTN_PAIR = 2 * ATTN_HEAD_DIM
ATTN_KEYS = 3 * WINDOW
ATTN_Q_TILE = 2 * ATTN_GROUP * ATTN_HEAD_DIM


def _rope(v, tc, ta, tb):
    n = v.shape[-1]
    half = ROPE_DIM // 2
    return v * tc + pltpu.roll(v, n - half, 1) * ta + pltpu.roll(v, half, 1) * tb


def _attn_kernel(q_ref, k_ref, v_ref, tcq_ref, taq_ref, tbq_ref, tck_ref, tak_ref, tbk_ref, sink_ref,
                 o_ref, kdup_ref, vlo_ref, vhi_ref):
    n = pl.program_id(2)
    n_kv = 2

    @pl.when(n == 0)
    def _():
        kr = _rope(k_ref[...], tck_ref[...], tak_ref[...], tbk_ref[...])
        vv = v_ref[...]
        kr_sw = pltpu.roll(kr, ATTN_HEAD_DIM, 1)
        vv_sw = pltpu.roll(vv, ATTN_HEAD_DIM, 1)
        lo = lax.broadcasted_iota(jnp.int32, (SEQ, ATTN_PAIR), 1) < ATTN_HEAD_DIM
        kdup_ref[0] = jnp.where(lo, kr, kr_sw).astype(BF16)
        kdup_ref[1] = jnp.where(lo, kr_sw, kr).astype(BF16)
        vlo_ref[0] = jnp.where(lo, vv, 0.0).astype(BF16)
        vlo_ref[1] = jnp.where(lo, vv_sw, 0.0).astype(BF16)
        vhi_ref[0] = jnp.where(lo, 0.0, vv_sw).astype(BF16)
        vhi_ref[1] = jnp.where(lo, 0.0, vv).astype(BF16)

    start = pl.multiple_of(jnp.clip((n - 1) * WINDOW, 0, SEQ - ATTN_KEYS), WINDOW)
    keys = pl.ds(start, ATTN_KEYS)
    n_stack = 2 * (ATTN_GROUP // 2)
    qpos = n * WINDOW + lax.broadcasted_iota(jnp.int32, (n_stack, WINDOW, ATTN_KEYS), 1)
    kpos = start + lax.broadcasted_iota(jnp.int32, (n_stack, WINDOW, ATTN_KEYS), 2)
    valid = (jnp.abs(qpos - kpos) <= WINDOW).reshape(n_stack * WINDOW, ATTN_KEYS)
    q_lo = lax.broadcasted_iota(jnp.int32, (WINDOW, ATTN_PAIR), 1) < ATTN_HEAD_DIM
    q_scale = (ATTN_HEAD_DIM ** -0.5) * LOG2_E

    for kv in range(n_kv):
        pairs = [kv * (ATTN_GROUP // 2) + i for i in range(ATTN_GROUP // 2)]
        qs = [_rope(q_ref[:, p * ATTN_PAIR:(p + 1) * ATTN_PAIR], tcq_ref[...], taq_ref[...], tbq_ref[...]) * q_scale
              for p in pairs]
        lhs = jnp.concatenate([jnp.where(q_lo, q, 0.0) for q in qs] + [jnp.where(q_lo, 0.0, q) for q in qs],
                              axis=0).astype(BF16)
        sink = jnp.concatenate(
            [jnp.broadcast_to(sink_ref[2 * p + half:2 * p + half + 1, :], (WINDOW, ATTN_PAIR))
             for half in range(2) for p in pairs], axis=0) * LOG2_E
        s = lax.dot_general(lhs, kdup_ref[kv, keys, :], (((1,), (1,)), ((), ())), preferred_element_type=F32)
        s = jnp.where(valid, s, -jnp.inf)
        m = jnp.maximum(jnp.broadcast_to(jnp.max(s, axis=-1, keepdims=True), sink.shape), sink)
        p_un = jnp.exp2(s - jnp.tile(m, (1, ATTN_KEYS // ATTN_PAIR))).astype(BF16)
        denom = jnp.dot(p_un, jnp.ones((ATTN_KEYS, ATTN_PAIR), BF16), preferred_element_type=F32)
        denom = denom + jnp.exp2(sink - m)
        half_rows = (ATTN_GROUP // 2) * WINDOW
        pv_lo = jnp.dot(p_un[:half_rows], vlo_ref[kv, keys, :], preferred_element_type=F32) / denom[:half_rows]
        pv_hi = jnp.dot(p_un[half_rows:], vhi_ref[kv, keys, :], preferred_element_type=F32) / denom[half_rows:]
        out = pv_lo + pv_hi
        for i, p in enumerate(pairs):
            o_ref[:, p * ATTN_PAIR:(p + 1) * ATTN_PAIR] = out[i * WINDOW:(i + 1) * WINDOW].astype(o_ref.dtype)


def window_attention(qkv, tables, sinks):
    t = qkv.shape[0]
    nb = t // SEQ
    nblk = SEQ // WINDOW
    n_pairs = ATTN_KV_HEADS // 2
    k_col0 = ATTN_Q_HEADS * ATTN_HEAD_DIM // ATTN_PAIR
    v_col0 = k_col0 + n_pairs
    tc, ta, tb = tables
    q_tab = pl.BlockSpec((WINDOW, ATTN_PAIR), lambda b, kp, n: (n, 0))
    k_tab = pl.BlockSpec((SEQ, ATTN_PAIR), lambda b, kp, n: (0, 0))
    return pl.pallas_call(
        _attn_kernel,
        grid=(nb, n_pairs, nblk),
        in_specs=[
            pl.BlockSpec((WINDOW, ATTN_Q_TILE), lambda b, kp, n: (b * nblk + n, kp)),
            pl.BlockSpec((SEQ, ATTN_PAIR), lambda b, kp, n: (b, k_col0 + kp)),
            pl.BlockSpec((SEQ, ATTN_PAIR), lambda b, kp, n: (b, v_col0 + kp)),
            q_tab, q_tab, q_tab, k_tab, k_tab, k_tab,
            pl.BlockSpec((None, 2 * ATTN_GROUP, 1), lambda b, kp, n: (kp, 0, 0)),
        ],
        out_specs=pl.BlockSpec((WINDOW, ATTN_Q_TILE), lambda b, kp, n: (b * nblk + n, kp)),
        out_shape=jax.ShapeDtypeStruct((t, ATTN_Q_HEADS * ATTN_HEAD_DIM), BF16),
        scratch_shapes=[pltpu.VMEM((2, SEQ, ATTN_PAIR), BF16)] * 3,
        compiler_params=_params("parallel", "parallel", "arbitrary"),
        name="window_attention",
    )(qkv, qkv, qkv, tc, ta, tb, tc, ta, tb, sinks.reshape(n_pairs, 2 * ATTN_GROUP, 1))


def _rope_lane_tables():
    half = ROPE_DIM // 2
    inv = 1.0 / (ROPE_THETA ** (jnp.arange(0, ROPE_DIM, 2, dtype=F32) / ROPE_DIM))
    ang = jnp.arange(SEQ, dtype=F32)[:, None] * inv[None, :]
    cos, sin = jnp.cos(ang), jnp.sin(ang)
    pad = ATTN_HEAD_DIM - ROPE_DIM
    tc = jnp.concatenate([cos, cos, jnp.ones((SEQ, pad), F32)], axis=1)
    ta = jnp.concatenate([-sin, jnp.zeros((SEQ, half + pad), F32)], axis=1)
    tb = jnp.concatenate([jnp.zeros((SEQ, half), F32), sin, jnp.zeros((SEQ, pad), F32)], axis=1)
    return tuple(jnp.tile(tab, (1, 2)) for tab in (tc, ta, tb))


def _ssd_layer(x, g_pre, g_post, w_in, conv_w, conv_b, dt_bias, a_log, d_skip, norm_w, w_out):
    w_main = w_in[:, :SSD_MAIN_DIM].astype(BF16)
    w_dt = w_in[:, SSD_MAIN_DIM:].astype(BF16)
    proj = norm_matmul(x, g_pre, w_main, tm=1024, tn=1024, out_dtype=BF16)
    dt_raw = norm_matmul(x, g_pre, w_dt, tm=1024, tn=w_dt.shape[1], out_dtype=F32)
    small = _ssd_small_params(dt_bias, a_log, d_skip)
    y = ssd_scan(proj, _ssd_dt_layout(dt_raw), conv_w, conv_b[None, :], *small, norm_w[None, :])
    return matmul_norm_res(y, w_out.astype(BF16), g_post, x, tm=1024, tk=1024)


def _fnet_layer(x, g_pre, g_post, w_out, dft):
    w_cs, cl, sl_neg = dft
    pq = fnet_chan(x, g_pre, w_cs, tm=1024)
    f = fnet_seq(cl, sl_neg, pq, tm=1024, tn=1024)
    return matmul_norm_res(f, w_out.astype(BF16), g_post, x, tm=1024, tk=D_MODEL)


def _attn_layer(x, g_pre, g_post, w_in, sinks, w_out, rope):
    qkv = norm_matmul(x, g_pre, w_in.astype(BF16), tm=1024, tn=1024, out_dtype=F32)
    a = window_attention(qkv, rope, sinks)
    return matmul_norm_res(a, w_out.astype(BF16), g_post, x, tm=1024, tk=D_MODEL)


def kernel(x_prompt, x_sample, norm_w, ffn_w_in, ffn_w_out, ssd_w_in, ssd_conv_w, ssd_conv_b, ssd_dt_bias,
           ssd_a_log, ssd_d, ssd_norm_w, ssd_w_out, fnet_w_out, attn_w_in, attn_sinks, attn_w_out):
    n_prompt = x_prompt.shape[0]
    assert x_prompt.shape[1:] == (SEQ, D_MODEL) and x_sample.shape[1:] == (SEQ, D_MODEL)
    x = jnp.concatenate([x_prompt, x_sample], axis=0).reshape(-1, D_MODEL)
    dft = _dft_tables()
    rope = _rope_lane_tables()
    for i in range(DEPTH):
        kind, j = i % N_MIXERS, i // N_MIXERS
        g = norm_w[i][:, None, :]
        if kind == 0:
            x = _ssd_layer(x, g[0], g[1], ssd_w_in[j], ssd_conv_w[j], ssd_conv_b[j], ssd_dt_bias[j],
                           ssd_a_log[j], ssd_d[j], ssd_norm_w[j], ssd_w_out[j])
        elif kind == 1:
            x = _fnet_layer(x, g[0], g[1], fnet_w_out[j], dft)
        else:
            x = _attn_layer(x, g[0], g[1], attn_w_in[j], attn_sinks[j], attn_w_out[j], rope)
        x = ffn(x, g[2], ffn_w_in[i].astype(BF16), ffn_w_out[i].astype(BF16), g[3], tm=1024, tf=512)
    x = x.reshape(-1, SEQ, D_MODEL)
    return (x[:n_prompt], x[n_prompt:])
```

```python
import functools

import numpy as np
import jax
import jax.numpy as jnp
from jax import lax
from jax.experimental import pallas as pl
from jax.experimental.pallas import tpu as pltpu

F32 = jnp.float32
BF16 = jnp.bfloat16

D_MODEL = 2048
SEQ = 2048
DEPTH = 4
N_MIXERS = 3
NORM_EPS = 1e-6

D_INNER = 2 * D_MODEL
SSD_HEAD_DIM = 64
SSD_HEADS = D_INNER // SSD_HEAD_DIM
SSD_GROUPS = 8
SSD_HEADS_PER_GROUP = SSD_HEADS // SSD_GROUPS
SSD_STATE = 128
SSD_CONV = 5
SSD_CHUNK = 128
SSD_GROUP_WIDTH = D_INNER // SSD_GROUPS
CONV_DIM = D_INNER + 2 * SSD_GROUPS * SSD_STATE
SSD_MAIN_DIM = D_INNER + CONV_DIM

FNET_GROUPS = 4
FNET_WIDTH = D_MODEL // FNET_GROUPS

ATTN_HEAD_DIM = 64
ATTN_Q_HEADS = D_MODEL // ATTN_HEAD_DIM
ATTN_KV_HEADS = 8
ATTN_GROUP = ATTN_Q_HEADS // ATTN_KV_HEADS
WINDOW = 128
ROPE_THETA = 500000.0
ROPE_DIM = ATTN_HEAD_DIM // 4
ATTN_IN_DIM = (ATTN_Q_HEADS + 2 * ATTN_KV_HEADS) * ATTN_HEAD_DIM

D_FF = ((8 * D_MODEL + 3 * 256 - 1) // (3 * 256)) * 256

VMEM_LIMIT_BYTES = 56 * 1024 * 1024


def _params(*semantics):
    return pltpu.CompilerParams(dimension_semantics=semantics, vmem_limit_bytes=VMEM_LIMIT_BYTES)


def _rms_scale(v, g):
    ms = jnp.mean(v * v, axis=-1, keepdims=True)
    return v * lax.rsqrt(ms + NORM_EPS) * g


def _norm_matmul_kernel(x_ref, g_ref, w_ref, *rest, has_tail):
    if has_tail:
        wt_ref, o_ref, ot_ref, h_ref = rest
    else:
        o_ref, h_ref = rest

    @pl.when(pl.program_id(1) == 0)
    def _():
        h_ref[...] = _rms_scale(x_ref[...], g_ref[...]).astype(BF16)
        if has_tail:
            ot_ref[...] = jnp.dot(h_ref[...], wt_ref[...], preferred_element_type=F32)

    o_ref[...] = jnp.dot(h_ref[...], w_ref[...], preferred_element_type=F32).astype(o_ref.dtype)


def norm_matmul(x, g, w, *, tm, tn, n_main, out_dtype):
    t, k = x.shape
    n_tail = w.shape[1] - n_main
    has_tail = n_tail > 0
    in_specs = [
        pl.BlockSpec((tm, k), lambda i, j: (i, 0)),
        pl.BlockSpec((1, k), lambda i, j: (0, 0)),
        pl.BlockSpec((k, tn), lambda i, j: (0, j)),
    ]
    out_specs = [pl.BlockSpec((tm, tn), lambda i, j: (i, j))]
    out_shape = [jax.ShapeDtypeStruct((t, n_main), out_dtype)]
    args = [x, g, w]
    if has_tail:
        assert n_main % n_tail == 0
        in_specs.append(pl.BlockSpec((k, n_tail), lambda i, j: (0, n_main // n_tail)))
        out_specs.append(pl.BlockSpec((tm, n_tail), lambda i, j: (i, 0)))
        out_shape.append(jax.ShapeDtypeStruct((t, n_tail), F32))
        args.append(w)
    outs = pl.pallas_call(
        functools.partial(_norm_matmul_kernel, has_tail=has_tail),
        grid=(t // tm, n_main // tn),
        in_specs=in_specs,
        out_specs=out_specs,
        out_shape=out_shape,
        scratch_shapes=[pltpu.VMEM((tm, k), BF16)],
        compiler_params=_params("parallel", "arbitrary"),
        name="norm_matmul",
    )(*args)
    return tuple(outs) if has_tail else outs[0]


def _matmul_norm_res_kernel(a_ref, w_ref, g_ref, x_ref, o_ref, *, nk):
    if nk == 1:
        o_ref[...] = jnp.dot(a_ref[...], w_ref[...], preferred_element_type=F32)
        o_ref[...] = x_ref[...] + _rms_scale(o_ref[...], g_ref[...])
        return
    k = pl.program_id(1)

    @pl.when(k == 0)
    def _():
        o_ref[...] = jnp.zeros_like(o_ref)

    o_ref[...] += jnp.dot(a_ref[...], w_ref[...], preferred_element_type=F32)

    @pl.when(k == nk - 1)
    def _():
        o_ref[...] = x_ref[...] + _rms_scale(o_ref[...], g_ref[...])


def matmul_norm_res(a, w, g, x, *, tm, tk):
    t, kdim = a.shape
    d = w.shape[1]
    nk = kdim // tk
    return pl.pallas_call(
        functools.partial(_matmul_norm_res_kernel, nk=nk),
        grid=(t // tm, nk),
        in_specs=[
            pl.BlockSpec((tm, tk), lambda i, k: (i, k)),
            pl.BlockSpec((tk, d), lambda i, k: (k, 0), pipeline_mode=pl.Buffered(1) if nk == 1 else None),
            pl.BlockSpec((1, d), lambda i, k: (0, 0)),
            pl.BlockSpec((tm, d), lambda i, k: (i, 0)),
        ],
        out_specs=pl.BlockSpec((tm, d), lambda i, k: (i, 0)),
        out_shape=jax.ShapeDtypeStruct((t, d), F32),
        compiler_params=_params("parallel", "arbitrary"),
        name="matmul_norm_res",
    )(a, w, g, x)


def _ffn_kernel(x_ref, g_in_ref, w_gate_ref, w_up_ref, w_out_ref, g_out_ref, o_ref, h_ref, *, nf):
    f = pl.program_id(1)

    @pl.when(f == 0)
    def _():
        h_ref[...] = _rms_scale(x_ref[...], g_in_ref[...]).astype(BF16)
        o_ref[...] = jnp.zeros_like(o_ref)

    h = h_ref[...]
    gate = jnp.dot(h, w_gate_ref[...], preferred_element_type=F32)
    up = jnp.dot(h, w_up_ref[...], preferred_element_type=F32)
    act = (gate * jax.nn.sigmoid(gate) * up).astype(BF16)
    o_ref[...] += jnp.dot(act, w_out_ref[...], preferred_element_type=F32)

    @pl.when(f == nf - 1)
    def _():
        o_ref[...] = x_ref[...] + _rms_scale(o_ref[...], g_out_ref[...])


def ffn_weight_blocks(w_in, tf):
    d = w_in.shape[0]
    return w_in.astype(BF16).reshape(d, -1, tf).transpose(1, 0, 2)


def ffn(x, g_in, w_in_blocks, w_out, g_out, *, tm):
    t, d = x.shape
    tf = w_in_blocks.shape[2]
    nf = D_FF // tf
    return pl.pallas_call(
        functools.partial(_ffn_kernel, nf=nf),
        grid=(t // tm, nf),
        in_specs=[
            pl.BlockSpec((tm, d), lambda i, f: (i, 0), pipeline_mode=pl.Buffered(1)),
            pl.BlockSpec((1, d), lambda i, f: (0, 0)),
            pl.BlockSpec((None, d, tf), lambda i, f: (f, 0, 0)),
            pl.BlockSpec((None, d, tf), lambda i, f: (nf + f, 0, 0)),
            pl.BlockSpec((tf, d), lambda i, f: (f, 0)),
            pl.BlockSpec((1, d), lambda i, f: (0, 0)),
        ],
        out_specs=pl.BlockSpec((tm, d), lambda i, f: (i, 0)),
        out_shape=jax.ShapeDtypeStruct((t, d), F32),
        scratch_shapes=[pltpu.VMEM((tm, d), BF16)],
        compiler_params=_params("parallel", "arbitrary"),
        name="ffn",
    )(x, g_in, w_in_blocks, w_in_blocks, w_out, g_out)


SSD_CONV_ROWS = 256
SSD_CONV_HALO = 16
SSD_DT_LANES = 2 * SSD_HEADS_PER_GROUP
SSD_PAIR = 2 * SSD_HEAD_DIM


def _conv_silu(src_ref, w_ref, b_ref, dst_ref):
    rows, halo = SSD_CONV_ROWS, SSD_CONV_HALO
    n_steps = SEQ // rows
    ext_rows = rows + 2 * halo
    pad = SSD_CONV // 2
    w = w_ref[...]
    bias = b_ref[...]

    def body(i, carry):
        r0 = pl.multiple_of(i * rows, rows)
        cur = src_ref[pl.ds(r0, rows), :].astype(F32)
        rp = pl.multiple_of(jnp.maximum(r0 - halo, 0), halo)
        rn = pl.multiple_of(jnp.minimum(r0 + rows, SEQ - halo), halo)
        prev = src_ref[pl.ds(rp, halo), :].astype(F32) * jnp.where(i > 0, 1.0, 0.0)
        nxt = src_ref[pl.ds(rn, halo), :].astype(F32) * jnp.where(i < n_steps - 1, 1.0, 0.0)
        ext = jnp.concatenate([prev, cur, nxt], axis=0)
        acc = cur * w[pad:pad + 1, :] + bias
        for d in range(-pad, pad + 1):
            if d == 0:
                continue
            shifted = pltpu.roll(ext, (ext_rows - d) % ext_rows, 0)[halo:halo + rows]
            acc = acc + shifted * w[pad + d:pad + d + 1, :]
        dst_ref[pl.ds(r0, rows), :] = (acc * jax.nn.sigmoid(acc)).astype(dst_ref.dtype)
        return carry

    lax.fori_loop(0, n_steps, body, 0)


def _prefix_sum(a, axis):
    n = a.shape[axis]
    idx = lax.broadcasted_iota(jnp.int32, a.shape, axis)
    s = 1
    while s < n:
        a = a + jnp.where(idx >= s, pltpu.roll(a, s, axis), 0.0)
        s *= 2
    return a


LOG2_E = float(np.log2(np.e))


def _ssd_token_tables(dtr_ref, biasr_ref, alogr_ref, bs_ref, w_ref, v_ref, ds_ref, et_ref, wcol_ref, bt_ref):
    q = SSD_CHUNK
    n_chunks = SEQ // q
    rows = n_chunks * SSD_DT_LANES
    dt = jax.nn.softplus(dtr_ref[...].reshape(rows, q) + biasr_ref[...])
    adt = dt * (-jnp.exp(alogr_ref[...]))
    pre = _prefix_sum(adt, 1)
    tot = pre[:, q - 1:q]
    row = lax.broadcasted_iota(jnp.int32, (rows, q), 0)
    backward = (row % SSD_DT_LANES) >= SSD_HEADS_PER_GROUP
    u = jnp.where(backward, tot - pre + adt, pre)
    w = u * LOG2_E
    w_ref[...] = w
    v_ref[...] = w - jnp.log2(dt)
    ds_ref[...] = dt * jnp.exp(tot - u)
    et_ref[...] = jnp.broadcast_to(jnp.exp(tot), (rows, q))

    def body(c, carry):
        r0 = pl.multiple_of(c * SSD_DT_LANES, SSD_DT_LANES)
        tile = jnp.concatenate([w_ref[pl.ds(r0, SSD_DT_LANES), :], jnp.zeros((q - SSD_DT_LANES, q), F32)], axis=0)
        wcol_ref[c] = tile.T
        t0 = pl.multiple_of(c * q, q)
        bt_ref[c] = bs_ref[pl.ds(t0, q), :].astype(F32).T
        return carry

    lax.fori_loop(0, n_chunks, body, 0)


def _ssd_chunk(c, backward, refs, st_ref):
    xs_ref, bs_ref, cs_ref, v_ref, ds_ref, et_ref, wcol_ref, bt_ref = refs
    q = SSD_CHUNK
    hpg = SSD_HEADS_PER_GROUP
    h0 = hpg * int(backward)
    t0 = pl.multiple_of(c * q, q)
    r0 = pl.multiple_of(c * SSD_DT_LANES + h0, hpg)

    v = v_ref[pl.ds(r0, hpg), :]
    dsr = ds_ref[pl.ds(r0, hpg), :]
    etot = et_ref[pl.ds(r0, hpg), :]
    w_col = wcol_ref[c]
    bt = bt_ref[c]

    xs = xs_ref[pl.ds(t0, q), :].astype(BF16)
    bc = bs_ref[pl.ds(t0, q), :]
    cc = cs_ref[pl.ds(t0, q), :]
    cb = lax.dot_general(cc, bc, (((1,), (1,)), ((), ())), preferred_element_type=F32)
    row_i = lax.broadcasted_iota(jnp.int32, (q, q), 0)
    col_j = lax.broadcasted_iota(jnp.int32, (q, q), 1)
    mask = (col_j >= row_i) if backward else (col_j <= row_i)
    lane_lo = lax.broadcasted_iota(jnp.int32, (q, SSD_PAIR), 1) < SSD_HEAD_DIM
    zero = jnp.zeros((q, SSD_PAIR), BF16)

    y_tiles = []
    for k in range(hpg // 2):
        cols = slice(k * SSD_PAIR, (k + 1) * SSD_PAIR)
        ms, bts, ods, ets = [], [], [], []
        for h in (2 * k, 2 * k + 1):
            w_i = jnp.broadcast_to(w_col[:, h0 + h:h0 + h + 1], (q, q))
            decay_dt = jnp.exp2(jnp.where(mask, w_i - v[h:h + 1, :], -jnp.inf))
            ms.append((cb * decay_dt).astype(BF16))
            bts.append((bt * dsr[h:h + 1, :]).astype(BF16))
            ods.append(jnp.exp2(w_i))
            ets.append(jnp.broadcast_to(etot[h:h + 1, :], (SSD_STATE, SSD_PAIR)))
        x2 = xs[:, cols]
        lhs = jnp.concatenate([jnp.concatenate(ms, axis=1), jnp.concatenate(bts, axis=1)], axis=0)
        rhs = jnp.concatenate([jnp.where(lane_lo, x2, zero), jnp.where(lane_lo, zero, x2)], axis=0)
        res = jnp.dot(lhs, rhs, preferred_element_type=F32)
        state = st_ref[:, cols]
        y_off = jnp.dot(cc, state.astype(BF16), preferred_element_type=F32)
        y_tiles.append(res[:q] + y_off * jnp.where(lane_lo, ods[0], ods[1]))
        st_ref[:, cols] = state * jnp.where(lane_lo, ets[0], ets[1]) + res[q:]
    return jnp.concatenate(y_tiles, axis=1)


def _ssd_kernel(z_ref, x_ref, b_ref, c_ref, dtr_ref,
                cwx_ref, cwb_ref, cwc_ref, cbx_ref, cbb_ref, cbc_ref,
                biasr_ref, alogr_ref, dskip_ref, nw_ref,
                o_ref, xs_ref, bs_ref, cs_ref, yf_ref, yb_ref, stf_ref, stb_ref,
                w_ref, v_ref, ds_ref, et_ref, wcol_ref, bt_ref):
    _conv_silu(x_ref, cwx_ref, cbx_ref, xs_ref)
    _conv_silu(b_ref, cwb_ref, cbb_ref, bs_ref)
    _conv_silu(c_ref, cwc_ref, cbc_ref, cs_ref)
    _ssd_token_tables(dtr_ref, biasr_ref, alogr_ref, bs_ref, w_ref, v_ref, ds_ref, et_ref, wcol_ref, bt_ref)
    refs = (xs_ref, bs_ref, cs_ref, v_ref, ds_ref, et_ref, wcol_ref, bt_ref)
    n_chunks = SEQ // SSD_CHUNK
    q = SSD_CHUNK
    stf_ref[...] = jnp.zeros_like(stf_ref)
    stb_ref[...] = jnp.zeros_like(stb_ref)

    def rows(c):
        return pl.ds(pl.multiple_of(c * q, q), q)

    def finish(c, y):
        y = y + dskip_ref[...] * xs_ref[rows(c), :]
        z = z_ref[rows(c), :].astype(F32)
        y = y * (z * jax.nn.sigmoid(z))
        o_ref[rows(c), :] = _rms_scale(y, nw_ref[...]).astype(o_ref.dtype)

    def first_half(t, carry):
        cf, cbk = t, n_chunks - 1 - t
        yf_ref[rows(cf), :] = _ssd_chunk(cf, False, refs, stf_ref)
        yb_ref[rows(cbk), :] = _ssd_chunk(cbk, True, refs, stb_ref)
        return carry

    def second_half(t, carry):
        cf, cbk = t, n_chunks - 1 - t
        finish(cf, _ssd_chunk(cf, False, refs, stf_ref) + yb_ref[rows(cf), :])
        finish(cbk, _ssd_chunk(cbk, True, refs, stb_ref) + yf_ref[rows(cbk), :])
        return carry

    lax.fori_loop(0, n_chunks // 2, first_half, 0, unroll=2)
    lax.fori_loop(n_chunks // 2, n_chunks, second_half, 0, unroll=2)


def ssd_scan(proj, dt_row, conv_w, conv_b, bias_row, alog_row, d_skip, norm_w):
    t = proj.shape[0]
    nb = t // SEQ
    gw, ns = SSD_GROUP_WIDTH, SSD_STATE
    g_blocks = D_INNER // gw
    b_col0 = (2 * D_INNER) // ns
    c_col0 = b_col0 + SSD_GROUPS
    n_chunks = SEQ // SSD_CHUNK
    table_rows = n_chunks * SSD_DT_LANES

    def cw(width, col0):
        return pl.BlockSpec((SSD_CONV, width), lambda b, g: (0, col0 + g))

    def cb(width, col0):
        return pl.BlockSpec((1, width), lambda b, g: (0, col0 + g))

    return pl.pallas_call(
        _ssd_kernel,
        grid=(nb, SSD_GROUPS),
        in_specs=[
            pl.BlockSpec((SEQ, gw), lambda b, g: (b, g)),
            pl.BlockSpec((SEQ, gw), lambda b, g: (b, g_blocks + g)),
            pl.BlockSpec((SEQ, ns), lambda b, g: (b, b_col0 + g)),
            pl.BlockSpec((SEQ, ns), lambda b, g: (b, c_col0 + g)),
            pl.BlockSpec((None, None, n_chunks, SSD_DT_LANES, SSD_CHUNK), lambda b, g: (b, g, 0, 0, 0)),
            cw(gw, 0), cw(ns, D_INNER // ns), cw(ns, D_INNER // ns + SSD_GROUPS),
            cb(gw, 0), cb(ns, D_INNER // ns), cb(ns, D_INNER // ns + SSD_GROUPS),
            pl.BlockSpec((None, table_rows, 1), lambda b, g: (g, 0, 0)),
            pl.BlockSpec((None, table_rows, 1), lambda b, g: (g, 0, 0)),
            pl.BlockSpec((1, gw), lambda b, g: (0, g)),
            pl.BlockSpec((1, gw), lambda b, g: (0, g)),
        ],
        out_specs=pl.BlockSpec((SEQ, gw), lambda b, g: (b, g)),
        out_shape=jax.ShapeDtypeStruct((t, D_INNER), BF16),
        scratch_shapes=[
            pltpu.VMEM((SEQ, gw), F32),
            pltpu.VMEM((SEQ, ns), BF16),
            pltpu.VMEM((SEQ, ns), BF16),
            pltpu.VMEM((SEQ, gw), F32),
            pltpu.VMEM((SEQ, gw), F32),
            pltpu.VMEM((ns, gw), F32),
            pltpu.VMEM((ns, gw), F32),
            pltpu.VMEM((table_rows, SSD_CHUNK), F32),
            pltpu.VMEM((table_rows, SSD_CHUNK), F32),
            pltpu.VMEM((table_rows, SSD_CHUNK), F32),
            pltpu.VMEM((table_rows, SSD_CHUNK), F32),
            pltpu.VMEM((n_chunks, SSD_CHUNK, SSD_CHUNK), F32),
            pltpu.VMEM((n_chunks, ns, SSD_CHUNK), F32),
        ],
        compiler_params=_params("parallel", "parallel"),
        name="ssd_scan",
    )(proj, proj, proj, proj, dt_row, conv_w, conv_w, conv_w, conv_b, conv_b, conv_b,
      bias_row, alog_row, d_skip, norm_w)


def _ssd_small_params(dt_bias, a_log, d_skip):
    def per_group(p):
        pg = p.reshape(2, SSD_GROUPS, SSD_HEADS_PER_GROUP).transpose(1, 0, 2)
        return jnp.tile(pg.reshape(SSD_GROUPS, SSD_DT_LANES, 1), (1, SEQ // SSD_CHUNK, 1))

    return per_group(dt_bias), per_group(a_log), jnp.repeat(d_skip, SSD_HEAD_DIM)[None, :]


def _ssd_dt_layout(dt_raw):
    nb = dt_raw.shape[0] // SEQ
    n_chunks = SEQ // SSD_CHUNK
    d5 = dt_raw.reshape(nb, n_chunks, SSD_CHUNK, 2, SSD_GROUPS, SSD_HEADS_PER_GROUP)
    return d5.transpose(0, 4, 1, 3, 5, 2).reshape(nb, SSD_GROUPS, n_chunks, SSD_DT_LANES, SSD_CHUNK)


def _fnet_chan_kernel(x_ref, g_ref, w_ref, o_ref):
    h = _rms_scale(x_ref[...], g_ref[...]).astype(BF16)
    d = x_ref.shape[1]
    for gi in range(FNET_GROUPS):
        cols = slice(gi * FNET_WIDTH, (gi + 1) * FNET_WIDTH)
        for kind in range(2):
            out = jnp.dot(h[:, cols], w_ref[kind], preferred_element_type=F32)
            o_ref[:, kind * d + gi * FNET_WIDTH:kind * d + (gi + 1) * FNET_WIDTH] = out.astype(o_ref.dtype)


def fnet_chan(x, g, w_cs, *, tm):
    t, d = x.shape
    return pl.pallas_call(
        _fnet_chan_kernel,
        grid=(t // tm,),
        in_specs=[
            pl.BlockSpec((tm, d), lambda i: (i, 0)),
            pl.BlockSpec((1, d), lambda i: (0, 0)),
            pl.BlockSpec((2, FNET_WIDTH, FNET_WIDTH), lambda i: (0, 0, 0)),
        ],
        out_specs=pl.BlockSpec((tm, 2 * d), lambda i: (i, 0)),
        out_shape=jax.ShapeDtypeStruct((t, 2 * d), BF16),
        compiler_params=_params("parallel"),
        name="fnet_chan",
    )(x, g, w_cs)


def _fnet_seq_kernel(cl_ref, sl_ref, p_ref, q_ref, o_ref):
    acc = jnp.dot(cl_ref[...], p_ref[...], preferred_element_type=F32)
    acc += jnp.dot(sl_ref[...], q_ref[...], preferred_element_type=F32)
    o_ref[...] = acc.astype(o_ref.dtype)


def fnet_seq(cl, sl_neg, pq, *, tm, tn):
    t = pq.shape[0]
    d = pq.shape[1] // 2
    nb = t // SEQ
    ni = SEQ // tm
    nj = d // tn
    return pl.pallas_call(
        _fnet_seq_kernel,
        grid=(nb, nj, ni),
        in_specs=[
            pl.BlockSpec((tm, SEQ), lambda b, j, i: (i, 0)),
            pl.BlockSpec((tm, SEQ), lambda b, j, i: (i, 0)),
            pl.BlockSpec((SEQ, tn), lambda b, j, i: (b, j)),
            pl.BlockSpec((SEQ, tn), lambda b, j, i: (b, nj + j)),
        ],
        out_specs=pl.BlockSpec((tm, tn), lambda b, j, i: (b * ni + i, j)),
        out_shape=jax.ShapeDtypeStruct((t, d), BF16),
        compiler_params=_params("parallel", "parallel", "arbitrary"),
        name="fnet_seq",
    )(cl, sl_neg, pq, pq)


def _dft_tables():
    def cos_sin(n):
        idx = jnp.arange(n, dtype=jnp.int32)
        ang = ((idx[:, None] * idx[None, :]) % n).astype(F32) * (2.0 * np.pi / n)
        return jnp.cos(ang), jnp.sin(ang)

    cc, sc = cos_sin(FNET_WIDTH)
    cl, sl = cos_sin(SEQ)
    scale = 1.0 / np.sqrt(SEQ * FNET_WIDTH)
    w_cs = jnp.stack([cc, sc]).astype(BF16)
    return w_cs, (cl * scale).astype(BF16), (-sl * scale).astype(BF16)


ATTN_PAIR = 2 * ATTN_HEAD_DIM
ATTN_KEYS = 3 * WINDOW
ATTN_Q_TILE = 2 * ATTN_GROUP * ATTN_HEAD_DIM


def _rope(v, tc, ta, tb):
    n = v.shape[-1]
    half = ROPE_DIM // 2
    return v * tc + pltpu.roll(v, n - half, 1) * ta + pltpu.roll(v, half, 1) * tb


def _attn_kernel(q_ref, k_ref, v_ref, tcq_ref, taq_ref, tbq_ref, tck_ref, tak_ref, tbk_ref, sink_ref,
                 o_ref, kdup_ref, vlo_ref, vhi_ref):
    n = pl.program_id(2)
    n_kv = 2

    @pl.when(n == 0)
    def _():
        kr = _rope(k_ref[...], tck_ref[...], tak_ref[...], tbk_ref[...])
        vv = v_ref[...]
        kr_sw = pltpu.roll(kr, ATTN_HEAD_DIM, 1)
        vv_sw = pltpu.roll(vv, ATTN_HEAD_DIM, 1)
        lo = lax.broadcasted_iota(jnp.int32, (SEQ, ATTN_PAIR), 1) < ATTN_HEAD_DIM
        kdup_ref[0] = jnp.where(lo, kr, kr_sw).astype(BF16)
        kdup_ref[1] = jnp.where(lo, kr_sw, kr).astype(BF16)
        vlo_ref[0] = jnp.where(lo, vv, 0.0).astype(BF16)
        vlo_ref[1] = jnp.where(lo, vv_sw, 0.0).astype(BF16)
        vhi_ref[0] = jnp.where(lo, 0.0, vv_sw).astype(BF16)
        vhi_ref[1] = jnp.where(lo, 0.0, vv).astype(BF16)

    start = pl.multiple_of(jnp.clip((n - 1) * WINDOW, 0, SEQ - ATTN_KEYS), WINDOW)
    keys = pl.ds(start, ATTN_KEYS)
    n_stack = 2 * (ATTN_GROUP // 2)
    qpos = n * WINDOW + lax.broadcasted_iota(jnp.int32, (n_stack, WINDOW, ATTN_KEYS), 1)
    kpos = start + lax.broadcasted_iota(jnp.int32, (n_stack, WINDOW, ATTN_KEYS), 2)
    valid = (jnp.abs(qpos - kpos) <= WINDOW).reshape(n_stack * WINDOW, ATTN_KEYS)
    q_lo = lax.broadcasted_iota(jnp.int32, (WINDOW, ATTN_PAIR), 1) < ATTN_HEAD_DIM
    q_scale = (ATTN_HEAD_DIM ** -0.5) * LOG2_E

    for kv in range(n_kv):
        pairs = [kv * (ATTN_GROUP // 2) + i for i in range(ATTN_GROUP // 2)]
        qs = [_rope(q_ref[:, p * ATTN_PAIR:(p + 1) * ATTN_PAIR], tcq_ref[...], taq_ref[...], tbq_ref[...]) * q_scale
              for p in pairs]
        lhs = jnp.concatenate([jnp.where(q_lo, q, 0.0) for q in qs] + [jnp.where(q_lo, 0.0, q) for q in qs],
                              axis=0).astype(BF16)
        sink = jnp.concatenate(
            [jnp.broadcast_to(sink_ref[2 * p + half:2 * p + half + 1, :], (WINDOW, ATTN_PAIR))
             for half in range(2) for p in pairs], axis=0) * LOG2_E
        s = lax.dot_general(lhs, kdup_ref[kv, keys, :], (((1,), (1,)), ((), ())), preferred_element_type=F32)
        s = jnp.where(valid, s, -jnp.inf)
        m = jnp.maximum(jnp.broadcast_to(jnp.max(s, axis=-1, keepdims=True), sink.shape), sink)
        p_un = jnp.exp2(s - jnp.tile(m, (1, ATTN_KEYS // ATTN_PAIR))).astype(BF16)
        denom = jnp.dot(p_un, jnp.ones((ATTN_KEYS, ATTN_PAIR), BF16), preferred_element_type=F32)
        denom = denom + jnp.exp2(sink - m)
        half_rows = (ATTN_GROUP // 2) * WINDOW
        pv_lo = jnp.dot(p_un[:half_rows], vlo_ref[kv, keys, :], preferred_element_type=F32) / denom[:half_rows]
        pv_hi = jnp.dot(p_un[half_rows:], vhi_ref[kv, keys, :], preferred_element_type=F32) / denom[half_rows:]
        out = pv_lo + pv_hi
        for i, p in enumerate(pairs):
            o_ref[:, p * ATTN_PAIR:(p + 1) * ATTN_PAIR] = out[i * WINDOW:(i + 1) * WINDOW].astype(o_ref.dtype)


def window_attention(qkv, tables, sinks):
    t = qkv.shape[0]
    nb = t // SEQ
    nblk = SEQ // WINDOW
    n_pairs = ATTN_KV_HEADS // 2
    k_col0 = ATTN_Q_HEADS * ATTN_HEAD_DIM // ATTN_PAIR
    v_col0 = k_col0 + n_pairs
    tc, ta, tb = tables
    q_tab = pl.BlockSpec((WINDOW, ATTN_PAIR), lambda b, kp, n: (n, 0))
    k_tab = pl.BlockSpec((SEQ, ATTN_PAIR), lambda b, kp, n: (0, 0))
    return pl.pallas_call(
        _attn_kernel,
        grid=(nb, n_pairs, nblk),
        in_specs=[
            pl.BlockSpec((WINDOW, ATTN_Q_TILE), lambda b, kp, n: (b * nblk + n, kp)),
            pl.BlockSpec((SEQ, ATTN_PAIR), lambda b, kp, n: (b, k_col0 + kp)),
            pl.BlockSpec((SEQ, ATTN_PAIR), lambda b, kp, n: (b, v_col0 + kp)),
            q_tab, q_tab, q_tab, k_tab, k_tab, k_tab,
            pl.BlockSpec((None, 2 * ATTN_GROUP, 1), lambda b, kp, n: (kp, 0, 0)),
        ],
        out_specs=pl.BlockSpec((WINDOW, ATTN_Q_TILE), lambda b, kp, n: (b * nblk + n, kp)),
        out_shape=jax.ShapeDtypeStruct((t, ATTN_Q_HEADS * ATTN_HEAD_DIM), BF16),
        scratch_shapes=[pltpu.VMEM((2, SEQ, ATTN_PAIR), BF16)] * 3,
        compiler_params=_params("parallel", "parallel", "arbitrary"),
        name="window_attention",
    )(qkv, qkv, qkv, tc, ta, tb, tc, ta, tb, sinks.reshape(n_pairs, 2 * ATTN_GROUP, 1))


def _rope_lane_tables():
    half = ROPE_DIM // 2
    inv = 1.0 / (ROPE_THETA ** (jnp.arange(0, ROPE_DIM, 2, dtype=F32) / ROPE_DIM))
    ang = jnp.arange(SEQ, dtype=F32)[:, None] * inv[None, :]
    cos, sin = jnp.cos(ang), jnp.sin(ang)
    pad = ATTN_HEAD_DIM - ROPE_DIM
    tc = jnp.concatenate([cos, cos, jnp.ones((SEQ, pad), F32)], axis=1)
    ta = jnp.concatenate([-sin, jnp.zeros((SEQ, half + pad), F32)], axis=1)
    tb = jnp.concatenate([jnp.zeros((SEQ, half), F32), sin, jnp.zeros((SEQ, pad), F32)], axis=1)
    return tuple(jnp.tile(tab, (1, 2)) for tab in (tc, ta, tb))


def _ssd_layer(x, g_pre, g_post, w_in, conv_w, conv_b, dt_bias, a_log, d_skip, norm_w, w_out):
    proj, dt_raw = norm_matmul(x, g_pre, w_in.astype(BF16), tm=1024, tn=1024, n_main=SSD_MAIN_DIM, out_dtype=BF16)
    small = _ssd_small_params(dt_bias, a_log, d_skip)
    y = ssd_scan(proj, _ssd_dt_layout(dt_raw), conv_w, conv_b[None, :], *small, norm_w[None, :])
    return matmul_norm_res(y, w_out.astype(BF16), g_post, x, tm=1024, tk=1024)


def _fnet_layer(x, g_pre, g_post, w_out, dft):
    w_cs, cl, sl_neg = dft
    pq = fnet_chan(x, g_pre, w_cs, tm=1024)
    f = fnet_seq(cl, sl_neg, pq, tm=1024, tn=1024)
    return matmul_norm_res(f, w_out.astype(BF16), g_post, x, tm=1024, tk=D_MODEL)


def _attn_layer(x, g_pre, g_post, w_in, sinks, w_out, rope):
    qkv = norm_matmul(x, g_pre, w_in.astype(BF16), tm=1024, tn=1024, n_main=ATTN_IN_DIM, out_dtype=F32)
    a = window_attention(qkv, rope, sinks)
    return matmul_norm_res(a, w_out.astype(BF16), g_post, x, tm=1024, tk=D_MODEL)


def kernel(x_prompt, x_sample, norm_w, ffn_w_in, ffn_w_out, ssd_w_in, ssd_conv_w, ssd_conv_b, ssd_dt_bias,
           ssd_a_log, ssd_d, ssd_norm_w, ssd_w_out, fnet_w_out, attn_w_in, attn_sinks, attn_w_out):
    n_prompt = x_prompt.shape[0]
    assert x_prompt.shape[1:] == (SEQ, D_MODEL) and x_sample.shape[1:] == (SEQ, D_MODEL)
    x = jnp.concatenate([x_prompt, x_sample], axis=0).reshape(-1, D_MODEL)
    dft = _dft_tables()
    rope = _rope_lane_tables()
    for i in range(DEPTH):
        kind, j = i % N_MIXERS, i // N_MIXERS
        g = norm_w[i][:, None, :]
        if kind == 0:
            x = _ssd_layer(x, g[0], g[1], ssd_w_in[j], ssd_conv_w[j], ssd_conv_b[j], ssd_dt_bias[j],
                           ssd_a_log[j], ssd_d[j], ssd_norm_w[j], ssd_w_out[j])
        elif kind == 1:
            x = _fnet_layer(x, g[0], g[1], fnet_w_out[j], dft)
        else:
            x = _attn_layer(x, g[0], g[1], attn_w_in[j], attn_sinks[j], attn_w_out[j], rope)
        x = ffn(x, g[2], ffn_weight_blocks(ffn_w_in[i], 512), ffn_w_out[i].astype(BF16), g[3], tm=1024)
    x = x.reshape(-1, SEQ, D_MODEL)
    return (x[:n_prompt], x[n_prompt:])
```

```python
import functools

import numpy as np
import jax
import jax.numpy as jnp
from jax import lax
from jax.experimental import pallas as pl
from jax.experimental.pallas import tpu as pltpu

F32 = jnp.float32
BF16 = jnp.bfloat16

D_MODEL = 2048
SEQ = 2048
DEPTH = 4
N_MIXERS = 3
NORM_EPS = 1e-6

D_INNER = 2 * D_MODEL
SSD_HEAD_DIM = 64
SSD_HEADS = D_INNER // SSD_HEAD_DIM
SSD_GROUPS = 8
SSD_HEADS_PER_GROUP = SSD_HEADS // SSD_GROUPS
SSD_STATE = 128
SSD_CONV = 5
SSD_CHUNK = 128
SSD_GROUP_WIDTH = D_INNER // SSD_GROUPS
CONV_DIM = D_INNER + 2 * SSD_GROUPS * SSD_STATE
SSD_MAIN_DIM = D_INNER + CONV_DIM

FNET_GROUPS = 4
FNET_WIDTH = D_MODEL // FNET_GROUPS

ATTN_HEAD_DIM = 64
ATTN_Q_HEADS = D_MODEL // ATTN_HEAD_DIM
ATTN_KV_HEADS = 8
ATTN_GROUP = ATTN_Q_HEADS // ATTN_KV_HEADS
WINDOW = 128
ROPE_THETA = 500000.0
ROPE_DIM = ATTN_HEAD_DIM // 4
ATTN_IN_DIM = (ATTN_Q_HEADS + 2 * ATTN_KV_HEADS) * ATTN_HEAD_DIM

D_FF = ((8 * D_MODEL + 3 * 256 - 1) // (3 * 256)) * 256

VMEM_LIMIT_BYTES = 56 * 1024 * 1024


def _params(*semantics):
    return pltpu.CompilerParams(dimension_semantics=semantics, vmem_limit_bytes=VMEM_LIMIT_BYTES)


def _rms_scale(v, g):
    ms = jnp.mean(v * v, axis=-1, keepdims=True)
    return v * lax.rsqrt(ms + NORM_EPS) * g


def _norm_matmul_kernel(x_ref, g_ref, w_ref, *rest, has_tail):
    if has_tail:
        wt_ref, o_ref, ot_ref, h_ref = rest
    else:
        o_ref, h_ref = rest

    @pl.when(pl.program_id(1) == 0)
    def _():
        h_ref[...] = _rms_scale(x_ref[...], g_ref[...]).astype(BF16)
        if has_tail:
            ot_ref[...] = jnp.dot(h_ref[...], wt_ref[...], preferred_element_type=F32)

    o_ref[...] = jnp.dot(h_ref[...], w_ref[...], preferred_element_type=F32).astype(o_ref.dtype)


def norm_matmul(x, g, w, layer, *, tm, tn, n_main, out_dtype):
    t, k = x.shape
    n_tail = w.shape[2] - n_main
    has_tail = n_tail > 0
    in_specs = [
        pl.BlockSpec((tm, k), lambda i, j: (i, 0)),
        pl.BlockSpec((1, k), lambda i, j: (0, 0)),
        pl.BlockSpec((None, k, tn), lambda i, j: (layer, 0, j)),
    ]
    out_specs = [pl.BlockSpec((tm, tn), lambda i, j: (i, j))]
    out_shape = [jax.ShapeDtypeStruct((t, n_main), out_dtype)]
    args = [x, g, w]
    if has_tail:
        assert n_main % n_tail == 0
        in_specs.append(pl.BlockSpec((None, k, n_tail), lambda i, j: (layer, 0, n_main // n_tail)))
        out_specs.append(pl.BlockSpec((tm, n_tail), lambda i, j: (i, 0)))
        out_shape.append(jax.ShapeDtypeStruct((t, n_tail), F32))
        args.append(w)
    outs = pl.pallas_call(
        functools.partial(_norm_matmul_kernel, has_tail=has_tail),
        grid=(t // tm, n_main // tn),
        in_specs=in_specs,
        out_specs=out_specs,
        out_shape=out_shape,
        scratch_shapes=[pltpu.VMEM((tm, k), BF16)],
        compiler_params=_params("parallel", "arbitrary"),
        name="norm_matmul",
    )(*args)
    return tuple(outs) if has_tail else outs[0]


def _matmul_norm_res_kernel(a_ref, w_ref, g_ref, x_ref, o_ref, *, nk):
    if nk == 1:
        o_ref[...] = jnp.dot(a_ref[...], w_ref[...], preferred_element_type=F32)
        o_ref[...] = x_ref[...] + _rms_scale(o_ref[...], g_ref[...])
        return
    k = pl.program_id(1)

    @pl.when(k == 0)
    def _():
        o_ref[...] = jnp.zeros_like(o_ref)

    o_ref[...] += jnp.dot(a_ref[...], w_ref[...], preferred_element_type=F32)

    @pl.when(k == nk - 1)
    def _():
        o_ref[...] = x_ref[...] + _rms_scale(o_ref[...], g_ref[...])


def matmul_norm_res(a, w, layer, g, x, *, tm, tk):
    t, kdim = a.shape
    d = w.shape[2]
    nk = kdim // tk
    return pl.pallas_call(
        functools.partial(_matmul_norm_res_kernel, nk=nk),
        grid=(t // tm, nk),
        in_specs=[
            pl.BlockSpec((tm, tk), lambda i, k: (i, k)),
            pl.BlockSpec((None, tk, d), lambda i, k: (layer, k, 0),
                         pipeline_mode=pl.Buffered(1) if nk == 1 else None),
            pl.BlockSpec((1, d), lambda i, k: (0, 0)),
            pl.BlockSpec((tm, d), lambda i, k: (i, 0)),
        ],
        out_specs=pl.BlockSpec((tm, d), lambda i, k: (i, 0)),
        out_shape=jax.ShapeDtypeStruct((t, d), F32),
        compiler_params=_params("parallel", "arbitrary"),
        name="matmul_norm_res",
    )(a, w, g, x)


def _ffn_kernel(x_ref, g_in_ref, w_gate_ref, w_up_ref, w_out_ref, g_out_ref, o_ref, h_ref, *, nf):
    f = pl.program_id(1)

    @pl.when(f == 0)
    def _():
        h_ref[...] = _rms_scale(x_ref[...], g_in_ref[...]).astype(BF16)
        o_ref[...] = jnp.zeros_like(o_ref)

    h = h_ref[...]
    gate = jnp.dot(h, w_gate_ref[...], preferred_element_type=F32)
    up = jnp.dot(h, w_up_ref[...], preferred_element_type=F32)
    act = (gate * jax.nn.sigmoid(gate) * up).astype(BF16)
    o_ref[...] += jnp.dot(act, w_out_ref[...], preferred_element_type=F32)

    @pl.when(f == nf - 1)
    def _():
        o_ref[...] = x_ref[...] + _rms_scale(o_ref[...], g_out_ref[...])


def ffn(x, g_in, w_in, w_out, layer, g_out, *, tm, tf, rows=None):
    d = x.shape[1]
    row0, row1 = rows if rows is not None else (0, x.shape[0])
    i0 = row0 // tm
    nf = D_FF // tf
    return pl.pallas_call(
        functools.partial(_ffn_kernel, nf=nf),
        grid=((row1 - row0) // tm, nf),
        in_specs=[
            pl.BlockSpec((tm, d), lambda i, f: (i0 + i, 0), pipeline_mode=pl.Buffered(1)),
            pl.BlockSpec((1, d), lambda i, f: (0, 0)),
            pl.BlockSpec((None, d, tf), lambda i, f: (layer, 0, f)),
            pl.BlockSpec((None, d, tf), lambda i, f: (layer, 0, nf + f)),
            pl.BlockSpec((None, tf, d), lambda i, f: (layer, f, 0)),
            pl.BlockSpec((1, d), lambda i, f: (0, 0)),
        ],
        out_specs=pl.BlockSpec((tm, d), lambda i, f: (i, 0)),
        out_shape=jax.ShapeDtypeStruct((row1 - row0, d), F32),
        scratch_shapes=[pltpu.VMEM((tm, d), BF16)],
        compiler_params=_params("parallel", "arbitrary"),
        name="ffn",
    )(x, g_in, w_in, w_in, w_out, g_out)


SSD_CONV_ROWS = 256
SSD_CONV_HALO = 16
SSD_DT_LANES = 2 * SSD_HEADS_PER_GROUP
SSD_PAIR = 2 * SSD_HEAD_DIM


def _conv_silu(src_ref, w_ref, b_ref, dst_ref):
    rows, halo = SSD_CONV_ROWS, SSD_CONV_HALO
    n_steps = SEQ // rows
    ext_rows = rows + 2 * halo
    pad = SSD_CONV // 2
    w = w_ref[...]
    bias = b_ref[...]

    def body(i, carry):
        r0 = pl.multiple_of(i * rows, rows)
        cur = src_ref[pl.ds(r0, rows), :].astype(F32)
        rp = pl.multiple_of(jnp.maximum(r0 - halo, 0), halo)
        rn = pl.multiple_of(jnp.minimum(r0 + rows, SEQ - halo), halo)
        prev = src_ref[pl.ds(rp, halo), :].astype(F32) * jnp.where(i > 0, 1.0, 0.0)
        nxt = src_ref[pl.ds(rn, halo), :].astype(F32) * jnp.where(i < n_steps - 1, 1.0, 0.0)
        ext = jnp.concatenate([prev, cur, nxt], axis=0)
        acc = cur * w[pad:pad + 1, :] + bias
        for d in range(-pad, pad + 1):
            if d == 0:
                continue
            shifted = pltpu.roll(ext, (ext_rows - d) % ext_rows, 0)[halo:halo + rows]
            acc = acc + shifted * w[pad + d:pad + d + 1, :]
        dst_ref[pl.ds(r0, rows), :] = (acc * jax.nn.sigmoid(acc)).astype(dst_ref.dtype)
        return carry

    lax.fori_loop(0, n_steps, body, 0)


def _prefix_sum(a, axis):
    n = a.shape[axis]
    idx = lax.broadcasted_iota(jnp.int32, a.shape, axis)
    s = 1
    while s < n:
        a = a + jnp.where(idx >= s, pltpu.roll(a, s, axis), 0.0)
        s *= 2
    return a


LOG2_E = float(np.log2(np.e))


def _ssd_token_tables(dtr_ref, biasr_ref, alogr_ref, xs_ref, bs_ref, cs_ref,
                      w_ref, v_ref, ds_ref, et_ref, wcol_ref, bt_ref, cb_ref, xpair_ref):
    q = SSD_CHUNK
    n_chunks = SEQ // q
    rows = n_chunks * SSD_DT_LANES
    dt = jax.nn.softplus(dtr_ref[...].reshape(rows, q) + biasr_ref[...])
    adt = dt * (-jnp.exp(alogr_ref[...]))
    pre = _prefix_sum(adt, 1)
    tot = pre[:, q - 1:q]
    row = lax.broadcasted_iota(jnp.int32, (rows, q), 0)
    backward = (row % SSD_DT_LANES) >= SSD_HEADS_PER_GROUP
    u = jnp.where(backward, tot - pre + adt, pre)
    w = u * LOG2_E
    w_ref[...] = w
    v_ref[...] = w - jnp.log2(dt)
    ds_ref[...] = dt * jnp.exp(tot - u)
    et_ref[...] = jnp.broadcast_to(jnp.exp(tot), (rows, q))

    lane_lo = lax.broadcasted_iota(jnp.int32, (q, SSD_PAIR), 1) < SSD_HEAD_DIM
    zero = jnp.zeros((q, SSD_PAIR), BF16)
    n_pairs = SSD_HEADS_PER_GROUP // 2

    def body(c, carry):
        r0 = pl.multiple_of(c * SSD_DT_LANES, SSD_DT_LANES)
        tile = jnp.concatenate([w_ref[pl.ds(r0, SSD_DT_LANES), :], jnp.zeros((q - SSD_DT_LANES, q), F32)], axis=0)
        wcol_ref[c] = tile.T
        t0 = pl.multiple_of(c * q, q)
        bc = bs_ref[pl.ds(t0, q), :]
        bt_ref[c] = bc.astype(F32).T
        cb_ref[c] = lax.dot_general(cs_ref[pl.ds(t0, q), :], bc, (((1,), (1,)), ((), ())),
                                    preferred_element_type=F32)
        xs = xs_ref[pl.ds(t0, q), :].astype(BF16)
        for k in range(n_pairs):
            x2 = xs[:, k * SSD_PAIR:(k + 1) * SSD_PAIR]
            xpair_ref[c * n_pairs + k] = jnp.concatenate(
                [jnp.where(lane_lo, x2, zero), jnp.where(lane_lo, zero, x2)], axis=0)
        return carry

    lax.fori_loop(0, n_chunks, body, 0)


def _ssd_chunk(c, backward, refs, st_ref):
    cs_ref, v_ref, ds_ref, et_ref, wcol_ref, bt_ref, cb_ref, xpair_ref = refs
    q = SSD_CHUNK
    hpg = SSD_HEADS_PER_GROUP
    h0 = hpg * int(backward)
    t0 = pl.multiple_of(c * q, q)
    r0 = pl.multiple_of(c * SSD_DT_LANES + h0, hpg)

    v = v_ref[pl.ds(r0, hpg), :]
    dsr = ds_ref[pl.ds(r0, hpg), :]
    etot = et_ref[pl.ds(r0, hpg), :]
    w_col = wcol_ref[c]
    bt = bt_ref[c]
    cb = cb_ref[c]
    cc = cs_ref[pl.ds(t0, q), :]
    row_i = lax.broadcasted_iota(jnp.int32, (q, q), 0)
    col_j = lax.broadcasted_iota(jnp.int32, (q, q), 1)
    mask = (col_j >= row_i) if backward else (col_j <= row_i)
    lane_lo = lax.broadcasted_iota(jnp.int32, (q, SSD_PAIR), 1) < SSD_HEAD_DIM

    y_tiles = []
    for k in range(hpg // 2):
        cols = slice(k * SSD_PAIR, (k + 1) * SSD_PAIR)
        ms, bts, ods, ets = [], [], [], []
        for h in (2 * k, 2 * k + 1):
            w_i = jnp.broadcast_to(w_col[:, h0 + h:h0 + h + 1], (q, q))
            decay_dt = jnp.exp2(jnp.where(mask, w_i - v[h:h + 1, :], -jnp.inf))
            ms.append((cb * decay_dt).astype(BF16))
            bts.append((bt * dsr[h:h + 1, :]).astype(BF16))
            ods.append(jnp.exp2(w_i))
            ets.append(jnp.broadcast_to(etot[h:h + 1, :], (SSD_STATE, SSD_PAIR)))
        lhs = jnp.concatenate([jnp.concatenate(ms, axis=1), jnp.concatenate(bts, axis=1)], axis=0)
        res = jnp.dot(lhs, xpair_ref[c * (hpg // 2) + k], preferred_element_type=F32)
        state = st_ref[:, cols]
        y_off = jnp.dot(cc, state.astype(BF16), preferred_element_type=F32)
        y_tiles.append(res[:q] + y_off * jnp.where(lane_lo, ods[0], ods[1]))
        st_ref[:, cols] = state * jnp.where(lane_lo, ets[0], ets[1]) + res[q:]
    return jnp.concatenate(y_tiles, axis=1)


def _ssd_kernel(z_ref, x_ref, b_ref, c_ref, dtr_ref,
                cwx_ref, cwb_ref, cwc_ref, cbx_ref, cbb_ref, cbc_ref,
                biasr_ref, alogr_ref, dskip_ref, nw_ref,
                o_ref, xs_ref, bs_ref, cs_ref, yf_ref, yb_ref, stf_ref, stb_ref,
                w_ref, v_ref, ds_ref, et_ref, wcol_ref, bt_ref, cb_ref, xpair_ref):
    _conv_silu(x_ref, cwx_ref, cbx_ref, xs_ref)
    _conv_silu(b_ref, cwb_ref, cbb_ref, bs_ref)
    _conv_silu(c_ref, cwc_ref, cbc_ref, cs_ref)
    _ssd_token_tables(dtr_ref, biasr_ref, alogr_ref, xs_ref, bs_ref, cs_ref,
                      w_ref, v_ref, ds_ref, et_ref, wcol_ref, bt_ref, cb_ref, xpair_ref)
    refs = (cs_ref, v_ref, ds_ref, et_ref, wcol_ref, bt_ref, cb_ref, xpair_ref)
    n_chunks = SEQ // SSD_CHUNK
    q = SSD_CHUNK
    stf_ref[...] = jnp.zeros_like(stf_ref)
    stb_ref[...] = jnp.zeros_like(stb_ref)

    def rows(c):
        return pl.ds(pl.multiple_of(c * q, q), q)

    def finish(c, y):
        y = y + dskip_ref[...] * xs_ref[rows(c), :]
        z = z_ref[rows(c), :].astype(F32)
        y = y * (z * jax.nn.sigmoid(z))
        o_ref[rows(c), :] = _rms_scale(y, nw_ref[...]).astype(o_ref.dtype)

    def first_half(t, carry):
        cf, cbk = t, n_chunks - 1 - t
        yf_ref[rows(cf), :] = _ssd_chunk(cf, False, refs, stf_ref)
        yb_ref[rows(cbk), :] = _ssd_chunk(cbk, True, refs, stb_ref)
        return carry

    def second_half(t, carry):
        cf, cbk = t, n_chunks - 1 - t
        finish(cf, _ssd_chunk(cf, False, refs, stf_ref) + yb_ref[rows(cf), :])
        finish(cbk, _ssd_chunk(cbk, True, refs, stb_ref) + yf_ref[rows(cbk), :])
        return carry

    lax.fori_loop(0, n_chunks // 2, first_half, 0, unroll=2)
    lax.fori_loop(n_chunks // 2, n_chunks, second_half, 0, unroll=2)


def ssd_scan(proj, dt_row, conv_w, conv_b, bias_row, alog_row, d_skip, norm_w):
    t = proj.shape[0]
    nb = t // SEQ
    gw, ns = SSD_GROUP_WIDTH, SSD_STATE
    g_blocks = D_INNER // gw
    b_col0 = (2 * D_INNER) // ns
    c_col0 = b_col0 + SSD_GROUPS
    n_chunks = SEQ // SSD_CHUNK
    table_rows = n_chunks * SSD_DT_LANES

    def cw(width, col0):
        return pl.BlockSpec((SSD_CONV, width), lambda b, g: (0, col0 + g))

    def cb(width, col0):
        return pl.BlockSpec((1, width), lambda b, g: (0, col0 + g))

    return pl.pallas_call(
        _ssd_kernel,
        grid=(nb, SSD_GROUPS),
        in_specs=[
            pl.BlockSpec((SEQ, gw), lambda b, g: (b, g)),
            pl.BlockSpec((SEQ, gw), lambda b, g: (b, g_blocks + g)),
            pl.BlockSpec((SEQ, ns), lambda b, g: (b, b_col0 + g)),
            pl.BlockSpec((SEQ, ns), lambda b, g: (b, c_col0 + g)),
            pl.BlockSpec((None, None, n_chunks, SSD_DT_LANES, SSD_CHUNK), lambda b, g: (b, g, 0, 0, 0)),
            cw(gw, 0), cw(ns, D_INNER // ns), cw(ns, D_INNER // ns + SSD_GROUPS),
            cb(gw, 0), cb(ns, D_INNER // ns), cb(ns, D_INNER // ns + SSD_GROUPS),
            pl.BlockSpec((None, table_rows, 1), lambda b, g: (g, 0, 0)),
            pl.BlockSpec((None, table_rows, 1), lambda b, g: (g, 0, 0)),
            pl.BlockSpec((1, gw), lambda b, g: (0, g)),
            pl.BlockSpec((1, gw), lambda b, g: (0, g)),
        ],
        out_specs=pl.BlockSpec((SEQ, gw), lambda b, g: (b, g)),
        out_shape=jax.ShapeDtypeStruct((t, D_INNER), BF16),
        scratch_shapes=[
            pltpu.VMEM((SEQ, gw), F32),
            pltpu.VMEM((SEQ, ns), BF16),
            pltpu.VMEM((SEQ, ns), BF16),
            pltpu.VMEM((SEQ, gw), F32),
            pltpu.VMEM((SEQ, gw), F32),
            pltpu.VMEM((ns, gw), F32),
            pltpu.VMEM((ns, gw), F32),
            pltpu.VMEM((table_rows, SSD_CHUNK), F32),
            pltpu.VMEM((table_rows, SSD_CHUNK), F32),
            pltpu.VMEM((table_rows, SSD_CHUNK), F32),
            pltpu.VMEM((table_rows, SSD_CHUNK), F32),
            pltpu.VMEM((n_chunks, SSD_CHUNK, SSD_CHUNK), F32),
            pltpu.VMEM((n_chunks, ns, SSD_CHUNK), F32),
            pltpu.VMEM((n_chunks, SSD_CHUNK, SSD_CHUNK), F32),
            pltpu.VMEM((n_chunks * gw // SSD_PAIR, 2 * SSD_CHUNK, SSD_PAIR), BF16),
        ],
        compiler_params=_params("parallel", "parallel"),
        name="ssd_scan",
    )(proj, proj, proj, proj, dt_row, conv_w, conv_w, conv_w, conv_b, conv_b, conv_b,
      bias_row, alog_row, d_skip, norm_w)


def _ssd_small_params(dt_bias, a_log, d_skip):
    def per_group(p):
        pg = p.reshape(2, SSD_GROUPS, SSD_HEADS_PER_GROUP).transpose(1, 0, 2)
        return jnp.tile(pg.reshape(SSD_GROUPS, SSD_DT_LANES, 1), (1, SEQ // SSD_CHUNK, 1))

    return per_group(dt_bias), per_group(a_log), jnp.repeat(d_skip, SSD_HEAD_DIM)[None, :]


def _ssd_dt_layout(dt_raw):
    nb = dt_raw.shape[0] // SEQ
    n_chunks = SEQ // SSD_CHUNK
    d5 = dt_raw.reshape(nb, n_chunks, SSD_CHUNK, 2, SSD_GROUPS, SSD_HEADS_PER_GROUP)
    return d5.transpose(0, 4, 1, 3, 5, 2).reshape(nb, SSD_GROUPS, n_chunks, SSD_DT_LANES, SSD_CHUNK)


def _fnet_chan_kernel(x_ref, g_ref, w_ref, o_ref):
    h = _rms_scale(x_ref[...], g_ref[...]).astype(BF16)
    d = x_ref.shape[1]
    for gi in range(FNET_GROUPS):
        cols = slice(gi * FNET_WIDTH, (gi + 1) * FNET_WIDTH)
        for kind in range(2):
            out = jnp.dot(h[:, cols], w_ref[kind], preferred_element_type=F32)
            o_ref[:, kind * d + gi * FNET_WIDTH:kind * d + (gi + 1) * FNET_WIDTH] = out.astype(o_ref.dtype)


def fnet_chan(x, g, w_cs, *, tm):
    t, d = x.shape
    return pl.pallas_call(
        _fnet_chan_kernel,
        grid=(t // tm,),
        in_specs=[
            pl.BlockSpec((tm, d), lambda i: (i, 0)),
            pl.BlockSpec((1, d), lambda i: (0, 0)),
            pl.BlockSpec((2, FNET_WIDTH, FNET_WIDTH), lambda i: (0, 0, 0)),
        ],
        out_specs=pl.BlockSpec((tm, 2 * d), lambda i: (i, 0)),
        out_shape=jax.ShapeDtypeStruct((t, 2 * d), BF16),
        compiler_params=_params("parallel"),
        name="fnet_chan",
    )(x, g, w_cs)


def _fnet_seq_kernel(cl_ref, sl_ref, p_ref, q_ref, o_ref):
    acc = jnp.dot(cl_ref[...], p_ref[...], preferred_element_type=F32)
    acc += jnp.dot(sl_ref[...], q_ref[...], preferred_element_type=F32)
    o_ref[...] = acc.astype(o_ref.dtype)


def fnet_seq(cl, sl_neg, pq, *, tm, tn):
    t = pq.shape[0]
    d = pq.shape[1] // 2
    nb = t // SEQ
    ni = SEQ // tm
    nj = d // tn
    return pl.pallas_call(
        _fnet_seq_kernel,
        grid=(nb, nj, ni),
        in_specs=[
            pl.BlockSpec((tm, SEQ), lambda b, j, i: (i, 0)),
            pl.BlockSpec((tm, SEQ), lambda b, j, i: (i, 0)),
            pl.BlockSpec((SEQ, tn), lambda b, j, i: (b, j)),
            pl.BlockSpec((SEQ, tn), lambda b, j, i: (b, nj + j)),
        ],
        out_specs=pl.BlockSpec((tm, tn), lambda b, j, i: (b * ni + i, j)),
        out_shape=jax.ShapeDtypeStruct((t, d), BF16),
        compiler_params=_params("parallel", "parallel", "arbitrary"),
        name="fnet_seq",
    )(cl, sl_neg, pq, pq)


def _dft_tables():
    def cos_sin(n):
        idx = jnp.arange(n, dtype=jnp.int32)
        ang = ((idx[:, None] * idx[None, :]) % n).astype(F32) * (2.0 * np.pi / n)
        return jnp.cos(ang), jnp.sin(ang)

    cc, sc = cos_sin(FNET_WIDTH)
    cl, sl = cos_sin(SEQ)
    scale = 1.0 / np.sqrt(SEQ * FNET_WIDTH)
    w_cs = jnp.stack([cc, sc]).astype(BF16)
    return w_cs, (cl * scale).astype(BF16), (-sl * scale).astype(BF16)


ATTN_PAIR = 2 * ATTN_HEAD_DIM
ATTN_KEYS = 3 * WINDOW
ATTN_Q_TILE = 2 * ATTN_GROUP * ATTN_HEAD_DIM


def _rope(v, tc, ta, tb):
    n = v.shape[-1]
    half = ROPE_DIM // 2
    return v * tc + pltpu.roll(v, n - half, 1) * ta + pltpu.roll(v, half, 1) * tb


def _attn_kernel(q_ref, k_ref, v_ref, tcq_ref, taq_ref, tbq_ref, tck_ref, tak_ref, tbk_ref, sink_ref,
                 o_ref, kdup_ref, vlo_ref, vhi_ref):
    n = pl.program_id(2)
    n_kv = 2

    @pl.when(n == 0)
    def _():
        kr = _rope(k_ref[...], tck_ref[...], tak_ref[...], tbk_ref[...])
        vv = v_ref[...]
        kr_sw = pltpu.roll(kr, ATTN_HEAD_DIM, 1)
        vv_sw = pltpu.roll(vv, ATTN_HEAD_DIM, 1)
        lo = lax.broadcasted_iota(jnp.int32, (SEQ, ATTN_PAIR), 1) < ATTN_HEAD_DIM
        kdup_ref[0] = jnp.where(lo, kr, kr_sw).astype(BF16)
        kdup_ref[1] = jnp.where(lo, kr_sw, kr).astype(BF16)
        vlo_ref[0] = jnp.where(lo, vv, 0.0).astype(BF16)
        vlo_ref[1] = jnp.where(lo, vv_sw, 0.0).astype(BF16)
        vhi_ref[0] = jnp.where(lo, 0.0, vv_sw).astype(BF16)
        vhi_ref[1] = jnp.where(lo, 0.0, vv).astype(BF16)

    start = pl.multiple_of(jnp.clip((n - 1) * WINDOW, 0, SEQ - ATTN_KEYS), WINDOW)
    keys = pl.ds(start, ATTN_KEYS)
    n_stack = 2 * (ATTN_GROUP // 2)
    qpos = n * WINDOW + lax.broadcasted_iota(jnp.int32, (n_stack, WINDOW, ATTN_KEYS), 1)
    kpos = start + lax.broadcasted_iota(jnp.int32, (n_stack, WINDOW, ATTN_KEYS), 2)
    valid = (jnp.abs(qpos - kpos) <= WINDOW).reshape(n_stack * WINDOW, ATTN_KEYS)
    q_lo = lax.broadcasted_iota(jnp.int32, (WINDOW, ATTN_PAIR), 1) < ATTN_HEAD_DIM
    q_scale = (ATTN_HEAD_DIM ** -0.5) * LOG2_E

    for kv in range(n_kv):
        pairs = [kv * (ATTN_GROUP // 2) + i for i in range(ATTN_GROUP // 2)]
        qs = [_rope(q_ref[:, p * ATTN_PAIR:(p + 1) * ATTN_PAIR], tcq_ref[...], taq_ref[...], tbq_ref[...]) * q_scale
              for p in pairs]
        lhs = jnp.concatenate([jnp.where(q_lo, q, 0.0) for q in qs] + [jnp.where(q_lo, 0.0, q) for q in qs],
                              axis=0).astype(BF16)
        sink = jnp.concatenate(
            [jnp.broadcast_to(sink_ref[2 * p + half:2 * p + half + 1, :], (WINDOW, ATTN_PAIR))
             for half in range(2) for p in pairs], axis=0) * LOG2_E
        s = lax.dot_general(lhs, kdup_ref[kv, keys, :], (((1,), (1,)), ((), ())), preferred_element_type=F32)
        s = jnp.where(valid, s, -jnp.inf)
        m = jnp.maximum(jnp.broadcast_to(jnp.max(s, axis=-1, keepdims=True), sink.shape), sink)
        p_un = jnp.exp2(s - jnp.tile(m, (1, ATTN_KEYS // ATTN_PAIR))).astype(BF16)
        denom = jnp.dot(p_un, jnp.ones((ATTN_KEYS, ATTN_PAIR), BF16), preferred_element_type=F32)
        denom = denom + jnp.exp2(sink - m)
        half_rows = (ATTN_GROUP // 2) * WINDOW
        pv_lo = jnp.dot(p_un[:half_rows], vlo_ref[kv, keys, :], preferred_element_type=F32) / denom[:half_rows]
        pv_hi = jnp.dot(p_un[half_rows:], vhi_ref[kv, keys, :], preferred_element_type=F32) / denom[half_rows:]
        out = pv_lo + pv_hi
        for i, p in enumerate(pairs):
            o_ref[:, p * ATTN_PAIR:(p + 1) * ATTN_PAIR] = out[i * WINDOW:(i + 1) * WINDOW].astype(o_ref.dtype)


def window_attention(qkv, tables, sinks):
    t = qkv.shape[0]
    nb = t // SEQ
    nblk = SEQ // WINDOW
    n_pairs = ATTN_KV_HEADS // 2
    k_col0 = ATTN_Q_HEADS * ATTN_HEAD_DIM // ATTN_PAIR
    v_col0 = k_col0 + n_pairs
    tc, ta, tb = tables
    q_tab = pl.BlockSpec((WINDOW, ATTN_PAIR), lambda b, kp, n: (n, 0))
    k_tab = pl.BlockSpec((SEQ, ATTN_PAIR), lambda b, kp, n: (0, 0))
    return pl.pallas_call(
        _attn_kernel,
        grid=(nb, n_pairs, nblk),
        in_specs=[
            pl.BlockSpec((WINDOW, ATTN_Q_TILE), lambda b, kp, n: (b * nblk + n, kp)),
            pl.BlockSpec((SEQ, ATTN_PAIR), lambda b, kp, n: (b, k_col0 + kp)),
            pl.BlockSpec((SEQ, ATTN_PAIR), lambda b, kp, n: (b, v_col0 + kp)),
            q_tab, q_tab, q_tab, k_tab, k_tab, k_tab,
            pl.BlockSpec((None, 2 * ATTN_GROUP, 1), lambda b, kp, n: (kp, 0, 0)),
        ],
        out_specs=pl.BlockSpec((WINDOW, ATTN_Q_TILE), lambda b, kp, n: (b * nblk + n, kp)),
        out_shape=jax.ShapeDtypeStruct((t, ATTN_Q_HEADS * ATTN_HEAD_DIM), BF16),
        scratch_shapes=[pltpu.VMEM((2, SEQ, ATTN_PAIR), BF16)] * 3,
        compiler_params=_params("parallel", "parallel", "arbitrary"),
        name="window_attention",
    )(qkv, qkv, qkv, tc, ta, tb, tc, ta, tb, sinks.reshape(n_pairs, 2 * ATTN_GROUP, 1))


def _rope_lane_tables():
    half = ROPE_DIM // 2
    inv = 1.0 / (ROPE_THETA ** (jnp.arange(0, ROPE_DIM, 2, dtype=F32) / ROPE_DIM))
    ang = jnp.arange(SEQ, dtype=F32)[:, None] * inv[None, :]
    cos, sin = jnp.cos(ang), jnp.sin(ang)
    pad = ATTN_HEAD_DIM - ROPE_DIM
    tc = jnp.concatenate([cos, cos, jnp.ones((SEQ, pad), F32)], axis=1)
    ta = jnp.concatenate([-sin, jnp.zeros((SEQ, half + pad), F32)], axis=1)
    tb = jnp.concatenate([jnp.zeros((SEQ, half), F32), sin, jnp.zeros((SEQ, pad), F32)], axis=1)
    return tuple(jnp.tile(tab, (1, 2)) for tab in (tc, ta, tb))


ROW_TILE = 1024
FFN_TILE = 512


def _ssd_layer(x, g_pre, g_post, w_in, j, conv_w, conv_b, dt_bias, a_log, d_skip, norm_w, w_out):
    proj, dt_raw = norm_matmul(x, g_pre, w_in, j, tm=ROW_TILE, tn=1024, n_main=SSD_MAIN_DIM, out_dtype=BF16)
    small = _ssd_small_params(dt_bias, a_log, d_skip)
    y = ssd_scan(proj, _ssd_dt_layout(dt_raw), conv_w, conv_b[None, :], *small, norm_w[None, :])
    return matmul_norm_res(y, w_out, j, g_post, x, tm=ROW_TILE, tk=1024)


def _fnet_layer(x, g_pre, g_post, w_out, j, dft):
    w_cs, cl, sl_neg = dft
    pq = fnet_chan(x, g_pre, w_cs, tm=ROW_TILE)
    f = fnet_seq(cl, sl_neg, pq, tm=1024, tn=1024)
    return matmul_norm_res(f, w_out, j, g_post, x, tm=ROW_TILE, tk=D_MODEL)


def _attn_layer(x, g_pre, g_post, w_in, j, sinks, w_out, rope):
    qkv = norm_matmul(x, g_pre, w_in, j, tm=ROW_TILE, tn=1024, n_main=ATTN_IN_DIM, out_dtype=F32)
    a = window_attention(qkv, rope, sinks)
    return matmul_norm_res(a, w_out, j, g_post, x, tm=ROW_TILE, tk=D_MODEL)


def kernel(x_prompt, x_sample, norm_w, ffn_w_in, ffn_w_out, ssd_w_in, ssd_conv_w, ssd_conv_b, ssd_dt_bias,
           ssd_a_log, ssd_d, ssd_norm_w, ssd_w_out, fnet_w_out, attn_w_in, attn_sinks, attn_w_out):
    n_prompt = x_prompt.shape[0]
    assert x_prompt.shape[1:] == (SEQ, D_MODEL) and x_sample.shape[1:] == (SEQ, D_MODEL)
    x = jnp.concatenate([x_prompt, x_sample], axis=0).reshape(-1, D_MODEL)
    dft = _dft_tables()
    rope = _rope_lane_tables()
    ffn_w_in, ffn_w_out, ssd_w_in, ssd_w_out, fnet_w_out, attn_w_in, attn_w_out = (
        w.astype(BF16) for w in (ffn_w_in, ffn_w_out, ssd_w_in, ssd_w_out, fnet_w_out, attn_w_in, attn_w_out))
    for i in range(DEPTH):
        kind, j = i % N_MIXERS, i // N_MIXERS
        g = norm_w[i][:, None, :]
        if kind == 0:
            x = _ssd_layer(x, g[0], g[1], ssd_w_in, j, ssd_conv_w[j], ssd_conv_b[j], ssd_dt_bias[j],
                           ssd_a_log[j], ssd_d[j], ssd_norm_w[j], ssd_w_out)
        elif kind == 1:
            x = _fnet_layer(x, g[0], g[1], fnet_w_out, j, dft)
        else:
            x = _attn_layer(x, g[0], g[1], attn_w_in, j, attn_sinks[j], attn_w_out, rope)
        if i < DEPTH - 1:
            x = ffn(x, g[2], ffn_w_in, ffn_w_out, i, g[3], tm=ROW_TILE, tf=FFN_TILE)
    split = n_prompt * SEQ
    last = DEPTH - 1
    g = norm_w[last][:, None, :]
    y_prompt = ffn(x, g[2], ffn_w_in, ffn_w_out, last, g[3], tm=ROW_TILE, tf=FFN_TILE, rows=(0, split))
    y_sample = ffn(x, g[2], ffn_w_in, ffn_w_out, last, g[3], tm=ROW_TILE, tf=FFN_TILE, rows=(split, x.shape[0]))
    return (y_prompt.reshape(-1, SEQ, D_MODEL), y_sample.reshape(-1, SEQ, D_MODEL))
```

```python
import functools

import numpy as np
import jax
import jax.numpy as jnp
from jax import lax
from jax.experimental import pallas as pl
from jax.experimental.pallas import tpu as pltpu

F32 = jnp.float32
BF16 = jnp.bfloat16

D_MODEL = 2048
SEQ = 2048
DEPTH = 4
N_MIXERS = 3
NORM_EPS = 1e-6

D_INNER = 2 * D_MODEL
SSD_HEAD_DIM = 64
SSD_HEADS = D_INNER // SSD_HEAD_DIM
SSD_GROUPS = 8
SSD_HEADS_PER_GROUP = SSD_HEADS // SSD_GROUPS
SSD_STATE = 128
SSD_CONV = 5
SSD_CHUNK = 128
SSD_GROUP_WIDTH = D_INNER // SSD_GROUPS
CONV_DIM = D_INNER + 2 * SSD_GROUPS * SSD_STATE
SSD_MAIN_DIM = D_INNER + CONV_DIM

FNET_GROUPS = 4
FNET_WIDTH = D_MODEL // FNET_GROUPS

ATTN_HEAD_DIM = 64
ATTN_Q_HEADS = D_MODEL // ATTN_HEAD_DIM
ATTN_KV_HEADS = 8
ATTN_GROUP = ATTN_Q_HEADS // ATTN_KV_HEADS
WINDOW = 128
ROPE_THETA = 500000.0
ROPE_DIM = ATTN_HEAD_DIM // 4
ATTN_IN_DIM = (ATTN_Q_HEADS + 2 * ATTN_KV_HEADS) * ATTN_HEAD_DIM

D_FF = ((8 * D_MODEL + 3 * 256 - 1) // (3 * 256)) * 256

VMEM_LIMIT_BYTES = 56 * 1024 * 1024


def _params(*semantics):
    return pltpu.CompilerParams(dimension_semantics=semantics, vmem_limit_bytes=VMEM_LIMIT_BYTES)


def _rms_scale(v, g):
    ms = jnp.mean(v * v, axis=-1, keepdims=True)
    return v * lax.rsqrt(ms + NORM_EPS) * g


def _norm_matmul_kernel(x_ref, g_ref, w_ref, *rest, has_tail):
    if has_tail:
        wt_ref, o_ref, ot_ref, h_ref = rest
    else:
        o_ref, h_ref = rest

    @pl.when(pl.program_id(1) == 0)
    def _():
        h_ref[...] = _rms_scale(x_ref[...], g_ref[...]).astype(BF16)
        if has_tail:
            ot_ref[...] = jnp.dot(h_ref[...], wt_ref[...], preferred_element_type=F32)

    o_ref[...] = jnp.dot(h_ref[...], w_ref[...], preferred_element_type=F32).astype(o_ref.dtype)


def norm_matmul(x, g, w, layer, *, tm, tn, n_main, out_dtype):
    t, k = x.shape
    n_tail = w.shape[2] - n_main
    has_tail = n_tail > 0
    in_specs = [
        pl.BlockSpec((tm, k), lambda i, j: (i, 0)),
        pl.BlockSpec((1, k), lambda i, j: (0, 0)),
        pl.BlockSpec((None, k, tn), lambda i, j: (layer, 0, j)),
    ]
    out_specs = [pl.BlockSpec((tm, tn), lambda i, j: (i, j))]
    out_shape = [jax.ShapeDtypeStruct((t, n_main), out_dtype)]
    args = [x, g, w]
    if has_tail:
        assert n_main % n_tail == 0
        in_specs.append(pl.BlockSpec((None, k, n_tail), lambda i, j: (layer, 0, n_main // n_tail)))
        out_specs.append(pl.BlockSpec((tm, n_tail), lambda i, j: (i, 0)))
        out_shape.append(jax.ShapeDtypeStruct((t, n_tail), F32))
        args.append(w)
    outs = pl.pallas_call(
        functools.partial(_norm_matmul_kernel, has_tail=has_tail),
        grid=(t // tm, n_main // tn),
        in_specs=in_specs,
        out_specs=out_specs,
        out_shape=out_shape,
        scratch_shapes=[pltpu.VMEM((tm, k), BF16)],
        compiler_params=_params("parallel", "arbitrary"),
        name="norm_matmul",
    )(*args)
    return tuple(outs) if has_tail else outs[0]


def _matmul_norm_res_kernel(a_ref, w_ref, g_ref, x_ref, o_ref, *, nk):
    if nk == 1:
        o_ref[...] = jnp.dot(a_ref[...], w_ref[...], preferred_element_type=F32)
        o_ref[...] = x_ref[...] + _rms_scale(o_ref[...], g_ref[...])
        return
    k = pl.program_id(1)

    @pl.when(k == 0)
    def _():
        o_ref[...] = jnp.zeros_like(o_ref)

    o_ref[...] += jnp.dot(a_ref[...], w_ref[...], preferred_element_type=F32)

    @pl.when(k == nk - 1)
    def _():
        o_ref[...] = x_ref[...] + _rms_scale(o_ref[...], g_ref[...])


def matmul_norm_res(a, w, layer, g, x, *, tm, tk):
    t, kdim = a.shape
    d = w.shape[2]
    nk = kdim // tk
    return pl.pallas_call(
        functools.partial(_matmul_norm_res_kernel, nk=nk),
        grid=(t // tm, nk),
        in_specs=[
            pl.BlockSpec((tm, tk), lambda i, k: (i, k)),
            pl.BlockSpec((None, tk, d), lambda i, k: (layer, k, 0),
                         pipeline_mode=pl.Buffered(1) if nk == 1 else None),
            pl.BlockSpec((1, d), lambda i, k: (0, 0)),
            pl.BlockSpec((tm, d), lambda i, k: (i, 0)),
        ],
        out_specs=pl.BlockSpec((tm, d), lambda i, k: (i, 0)),
        out_shape=jax.ShapeDtypeStruct((t, d), F32),
        compiler_params=_params("parallel", "arbitrary"),
        name="matmul_norm_res",
    )(a, w, g, x)


def _ffn_kernel(x_ref, g_in_ref, w_gate_ref, w_up_ref, w_out_ref, g_out_ref, o_ref, h_ref, *, nf):
    f = pl.program_id(1)

    @pl.when(f == 0)
    def _():
        h_ref[...] = _rms_scale(x_ref[...], g_in_ref[...]).astype(BF16)
        o_ref[...] = jnp.zeros_like(o_ref)

    h = h_ref[...]
    gate = jnp.dot(h, w_gate_ref[...], preferred_element_type=F32)
    up = jnp.dot(h, w_up_ref[...], preferred_element_type=F32)
    act = (gate * jax.nn.sigmoid(gate) * up).astype(BF16)
    o_ref[...] += jnp.dot(act, w_out_ref[...], preferred_element_type=F32)

    @pl.when(f == nf - 1)
    def _():
        o_ref[...] = x_ref[...] + _rms_scale(o_ref[...], g_out_ref[...])


def ffn(x, g_in, w_in, w_out, layer, g_out, *, tm, tf, rows=None):
    d = x.shape[1]
    row0, row1 = rows if rows is not None else (0, x.shape[0])
    i0 = row0 // tm
    nf = D_FF // tf
    return pl.pallas_call(
        functools.partial(_ffn_kernel, nf=nf),
        grid=((row1 - row0) // tm, nf),
        in_specs=[
            pl.BlockSpec((tm, d), lambda i, f: (i0 + i, 0), pipeline_mode=pl.Buffered(1)),
            pl.BlockSpec((1, d), lambda i, f: (0, 0)),
            pl.BlockSpec((None, d, tf), lambda i, f: (layer, 0, f)),
            pl.BlockSpec((None, d, tf), lambda i, f: (layer, 0, nf + f)),
            pl.BlockSpec((None, tf, d), lambda i, f: (layer, f, 0)),
            pl.BlockSpec((1, d), lambda i, f: (0, 0)),
        ],
        out_specs=pl.BlockSpec((tm, d), lambda i, f: (i, 0)),
        out_shape=jax.ShapeDtypeStruct((row1 - row0, d), F32),
        scratch_shapes=[pltpu.VMEM((tm, d), BF16)],
        compiler_params=_params("parallel", "arbitrary"),
        name="ffn",
    )(x, g_in, w_in, w_in, w_out, g_out)


SSD_CONV_ROWS = 256
SSD_CONV_HALO = 16
SSD_DT_LANES = 2 * SSD_HEADS_PER_GROUP
SSD_PAIR = 2 * SSD_HEAD_DIM


def _conv_silu(src_ref, w_ref, b_ref, dst_ref):
    rows, halo = SSD_CONV_ROWS, SSD_CONV_HALO
    n_steps = SEQ // rows
    ext_rows = rows + 2 * halo
    pad = SSD_CONV // 2
    w = w_ref[...]
    bias = b_ref[...]

    def body(i, carry):
        r0 = pl.multiple_of(i * rows, rows)
        cur = src_ref[pl.ds(r0, rows), :].astype(F32)
        rp = pl.multiple_of(jnp.maximum(r0 - halo, 0), halo)
        rn = pl.multiple_of(jnp.minimum(r0 + rows, SEQ - halo), halo)
        prev = src_ref[pl.ds(rp, halo), :].astype(F32) * jnp.where(i > 0, 1.0, 0.0)
        nxt = src_ref[pl.ds(rn, halo), :].astype(F32) * jnp.where(i < n_steps - 1, 1.0, 0.0)
        ext = jnp.concatenate([prev, cur, nxt], axis=0)
        acc = cur * w[pad:pad + 1, :] + bias
        for d in range(-pad, pad + 1):
            if d == 0:
                continue
            shifted = pltpu.roll(ext, (ext_rows - d) % ext_rows, 0)[halo:halo + rows]
            acc = acc + shifted * w[pad + d:pad + d + 1, :]
        dst_ref[pl.ds(r0, rows), :] = (acc * jax.nn.sigmoid(acc)).astype(dst_ref.dtype)
        return carry

    lax.fori_loop(0, n_steps, body, 0)


def _prefix_sum(a, axis):
    n = a.shape[axis]
    idx = lax.broadcasted_iota(jnp.int32, a.shape, axis)
    s = 1
    while s < n:
        a = a + jnp.where(idx >= s, pltpu.roll(a, s, axis), 0.0)
        s *= 2
    return a


LOG2_E = float(np.log2(np.e))


def _ssd_token_tables(dtr_ref, biasr_ref, alogr_ref, xs_ref, bs_ref, cs_ref,
                      w_ref, v_ref, ds_ref, et_ref, wcol_ref, bt_ref, cb_ref, xpair_ref):
    q = SSD_CHUNK
    n_chunks = SEQ // q
    rows = n_chunks * SSD_DT_LANES
    dt = jax.nn.softplus(dtr_ref[...].reshape(rows, q) + biasr_ref[...])
    adt = dt * (-jnp.exp(alogr_ref[...]))
    pre = _prefix_sum(adt, 1)
    tot = pre[:, q - 1:q]
    row = lax.broadcasted_iota(jnp.int32, (rows, q), 0)
    backward = (row % SSD_DT_LANES) >= SSD_HEADS_PER_GROUP
    u = jnp.where(backward, tot - pre + adt, pre)
    w = u * LOG2_E
    w_ref[...] = w
    v_ref[...] = w - jnp.log2(dt)
    ds_ref[...] = dt * jnp.exp(tot - u)
    et_ref[...] = jnp.broadcast_to(jnp.exp(tot), (rows, q))

    lane_lo = lax.broadcasted_iota(jnp.int32, (q, SSD_PAIR), 1) < SSD_HEAD_DIM
    zero = jnp.zeros((q, SSD_PAIR), BF16)
    n_pairs = SSD_HEADS_PER_GROUP // 2

    def body(c, carry):
        r0 = pl.multiple_of(c * SSD_DT_LANES, SSD_DT_LANES)
        tile = jnp.concatenate([w_ref[pl.ds(r0, SSD_DT_LANES), :], jnp.zeros((q - SSD_DT_LANES, q), F32)], axis=0)
        wcol_ref[c] = tile.T
        t0 = pl.multiple_of(c * q, q)
        bc = bs_ref[pl.ds(t0, q), :]
        bt_ref[c] = bc.astype(F32).T.astype(BF16)
        cb_ref[c] = lax.dot_general(cs_ref[pl.ds(t0, q), :], bc, (((1,), (1,)), ((), ())),
                                    preferred_element_type=F32).astype(BF16)
        xs = xs_ref[pl.ds(t0, q), :].astype(BF16)
        for k in range(n_pairs):
            x2 = xs[:, k * SSD_PAIR:(k + 1) * SSD_PAIR]
            xpair_ref[c * n_pairs + k] = jnp.concatenate(
                [jnp.where(lane_lo, x2, zero), jnp.where(lane_lo, zero, x2)], axis=0)
        return carry

    lax.fori_loop(0, n_chunks, body, 0, unroll=4)


def _ssd_chunk(c, backward, refs, st_ref):
    cs_ref, v_ref, ds_ref, et_ref, wcol_ref, bt_ref, cb_ref, xpair_ref = refs
    q = SSD_CHUNK
    hpg = SSD_HEADS_PER_GROUP
    h0 = hpg * int(backward)
    t0 = pl.multiple_of(c * q, q)
    r0 = pl.multiple_of(c * SSD_DT_LANES + h0, hpg)

    v = v_ref[pl.ds(r0, hpg), :]
    dsr = ds_ref[pl.ds(r0, hpg), :]
    etot = et_ref[pl.ds(r0, hpg), :]
    w_col = wcol_ref[c]
    bt = bt_ref[c]
    cb = cb_ref[c]
    cc = cs_ref[pl.ds(t0, q), :]
    row_i = lax.broadcasted_iota(jnp.int32, (q, q), 0)
    col_j = lax.broadcasted_iota(jnp.int32, (q, q), 1)
    mask = (col_j >= row_i) if backward else (col_j <= row_i)
    lane_lo = lax.broadcasted_iota(jnp.int32, (q, SSD_PAIR), 1) < SSD_HEAD_DIM

    y_tiles = []
    for k in range(hpg // 2):
        cols = slice(k * SSD_PAIR, (k + 1) * SSD_PAIR)
        ms, bts, ods, ets = [], [], [], []
        for h in (2 * k, 2 * k + 1):
            w_i = jnp.broadcast_to(w_col[:, h0 + h:h0 + h + 1], (q, q))
            decay_dt = jnp.exp2(jnp.where(mask, w_i - v[h:h + 1, :], -jnp.inf))
            ms.append(cb * decay_dt.astype(BF16))
            bts.append(bt * jnp.broadcast_to(dsr[h:h + 1, :], (SSD_STATE, q)).astype(BF16))
            ods.append(jnp.exp2(w_i))
            ets.append(jnp.broadcast_to(etot[h:h + 1, :], (SSD_STATE, SSD_PAIR)))
        lhs = jnp.concatenate([jnp.concatenate(ms, axis=1), jnp.concatenate(bts, axis=1)], axis=0)
        res = jnp.dot(lhs, xpair_ref[c * (hpg // 2) + k], preferred_element_type=F32)
        state = st_ref[:, cols]
        y_off = jnp.dot(cc, state.astype(BF16), preferred_element_type=F32)
        y_tiles.append(res[:q] + y_off * jnp.where(lane_lo, ods[0], ods[1]))
        st_ref[:, cols] = state * jnp.where(lane_lo, ets[0], ets[1]) + res[q:]
    return jnp.concatenate(y_tiles, axis=1)


def _ssd_kernel(z_ref, x_ref, b_ref, c_ref, dtr_ref,
                cwx_ref, cwb_ref, cwc_ref, cbx_ref, cbb_ref, cbc_ref,
                biasr_ref, alogr_ref, dskip_ref, nw_ref,
                o_ref, xs_ref, bs_ref, cs_ref, yf_ref, yb_ref, stf_ref, stb_ref,
                w_ref, v_ref, ds_ref, et_ref, wcol_ref, bt_ref, cb_ref, xpair_ref):
    _conv_silu(x_ref, cwx_ref, cbx_ref, xs_ref)
    _conv_silu(b_ref, cwb_ref, cbb_ref, bs_ref)
    _conv_silu(c_ref, cwc_ref, cbc_ref, cs_ref)
    _ssd_token_tables(dtr_ref, biasr_ref, alogr_ref, xs_ref, bs_ref, cs_ref,
                      w_ref, v_ref, ds_ref, et_ref, wcol_ref, bt_ref, cb_ref, xpair_ref)
    refs = (cs_ref, v_ref, ds_ref, et_ref, wcol_ref, bt_ref, cb_ref, xpair_ref)
    n_chunks = SEQ // SSD_CHUNK
    q = SSD_CHUNK
    stf_ref[...] = jnp.zeros_like(stf_ref)
    stb_ref[...] = jnp.zeros_like(stb_ref)

    def rows(c):
        return pl.ds(pl.multiple_of(c * q, q), q)

    def finish(c, y):
        y = y + dskip_ref[...] * xs_ref[rows(c), :]
        z = z_ref[rows(c), :].astype(F32)
        y = y * (z * jax.nn.sigmoid(z))
        o_ref[rows(c), :] = _rms_scale(y, nw_ref[...]).astype(o_ref.dtype)

    def first_half(t, carry):
        cf, cbk = t, n_chunks - 1 - t
        yf_ref[rows(cf), :] = _ssd_chunk(cf, False, refs, stf_ref)
        yb_ref[rows(cbk), :] = _ssd_chunk(cbk, True, refs, stb_ref)
        return carry

    def second_half(t, carry):
        cf, cbk = t, n_chunks - 1 - t
        finish(cf, _ssd_chunk(cf, False, refs, stf_ref) + yb_ref[rows(cf), :])
        finish(cbk, _ssd_chunk(cbk, True, refs, stb_ref) + yf_ref[rows(cbk), :])
        return carry

    lax.fori_loop(0, n_chunks // 2, first_half, 0, unroll=2)
    lax.fori_loop(n_chunks // 2, n_chunks, second_half, 0, unroll=2)


def ssd_scan(proj, dt_row, conv_w, conv_b, bias_row, alog_row, d_skip, norm_w):
    t = proj.shape[0]
    nb = t // SEQ
    gw, ns = SSD_GROUP_WIDTH, SSD_STATE
    g_blocks = D_INNER // gw
    b_col0 = (2 * D_INNER) // ns
    c_col0 = b_col0 + SSD_GROUPS
    n_chunks = SEQ // SSD_CHUNK
    table_rows = n_chunks * SSD_DT_LANES

    def cw(width, col0):
        return pl.BlockSpec((SSD_CONV, width), lambda b, g: (0, col0 + g))

    def cb(width, col0):
        return pl.BlockSpec((1, width), lambda b, g: (0, col0 + g))

    return pl.pallas_call(
        _ssd_kernel,
        grid=(nb, SSD_GROUPS),
        in_specs=[
            pl.BlockSpec((SEQ, gw), lambda b, g: (b, g)),
            pl.BlockSpec((SEQ, gw), lambda b, g: (b, g_blocks + g)),
            pl.BlockSpec((SEQ, ns), lambda b, g: (b, b_col0 + g)),
            pl.BlockSpec((SEQ, ns), lambda b, g: (b, c_col0 + g)),
            pl.BlockSpec((None, None, n_chunks, SSD_DT_LANES, SSD_CHUNK), lambda b, g: (b, g, 0, 0, 0)),
            cw(gw, 0), cw(ns, D_INNER // ns), cw(ns, D_INNER // ns + SSD_GROUPS),
            cb(gw, 0), cb(ns, D_INNER // ns), cb(ns, D_INNER // ns + SSD_GROUPS),
            pl.BlockSpec((None, table_rows, 1), lambda b, g: (g, 0, 0)),
            pl.BlockSpec((None, table_rows, 1), lambda b, g: (g, 0, 0)),
            pl.BlockSpec((1, gw), lambda b, g: (0, g)),
            pl.BlockSpec((1, gw), lambda b, g: (0, g)),
        ],
        out_specs=pl.BlockSpec((SEQ, gw), lambda b, g: (b, g)),
        out_shape=jax.ShapeDtypeStruct((t, D_INNER), BF16),
        scratch_shapes=[
            pltpu.VMEM((SEQ, gw), F32),
            pltpu.VMEM((SEQ, ns), BF16),
            pltpu.VMEM((SEQ, ns), BF16),
            pltpu.VMEM((SEQ, gw), F32),
            pltpu.VMEM((SEQ, gw), F32),
            pltpu.VMEM((ns, gw), F32),
            pltpu.VMEM((ns, gw), F32),
            pltpu.VMEM((table_rows, SSD_CHUNK), F32),
            pltpu.VMEM((table_rows, SSD_CHUNK), F32),
            pltpu.VMEM((table_rows, SSD_CHUNK), F32),
            pltpu.VMEM((table_rows, SSD_CHUNK), F32),
            pltpu.VMEM((n_chunks, SSD_CHUNK, SSD_CHUNK), F32),
            pltpu.VMEM((n_chunks, ns, SSD_CHUNK), BF16),
            pltpu.VMEM((n_chunks, SSD_CHUNK, SSD_CHUNK), BF16),
            pltpu.VMEM((n_chunks * gw // SSD_PAIR, 2 * SSD_CHUNK, SSD_PAIR), BF16),
        ],
        compiler_params=_params("parallel", "parallel"),
        name="ssd_scan",
    )(proj, proj, proj, proj, dt_row, conv_w, conv_w, conv_w, conv_b, conv_b, conv_b,
      bias_row, alog_row, d_skip, norm_w)


def _ssd_small_params(dt_bias, a_log, d_skip):
    def per_group(p):
        pg = p.reshape(2, SSD_GROUPS, SSD_HEADS_PER_GROUP).transpose(1, 0, 2)
        return jnp.tile(pg.reshape(SSD_GROUPS, SSD_DT_LANES, 1), (1, SEQ // SSD_CHUNK, 1))

    return per_group(dt_bias), per_group(a_log), jnp.repeat(d_skip, SSD_HEAD_DIM)[None, :]


def _ssd_dt_layout(dt_raw):
    nb = dt_raw.shape[0] // SEQ
    n_chunks = SEQ // SSD_CHUNK
    d5 = dt_raw.reshape(nb, n_chunks, SSD_CHUNK, 2, SSD_GROUPS, SSD_HEADS_PER_GROUP)
    return d5.transpose(0, 4, 1, 3, 5, 2).reshape(nb, SSD_GROUPS, n_chunks, SSD_DT_LANES, SSD_CHUNK)


def _fnet_chan_kernel(x_ref, g_ref, w_ref, o_ref):
    h = _rms_scale(x_ref[...], g_ref[...]).astype(BF16)
    d = x_ref.shape[1]
    for gi in range(FNET_GROUPS):
        cols = slice(gi * FNET_WIDTH, (gi + 1) * FNET_WIDTH)
        for kind in range(2):
            out = jnp.dot(h[:, cols], w_ref[kind], preferred_element_type=F32)
            o_ref[:, kind * d + gi * FNET_WIDTH:kind * d + (gi + 1) * FNET_WIDTH] = out.astype(o_ref.dtype)


def fnet_chan(x, g, w_cs, *, tm):
    t, d = x.shape
    return pl.pallas_call(
        _fnet_chan_kernel,
        grid=(t // tm,),
        in_specs=[
            pl.BlockSpec((tm, d), lambda i: (i, 0)),
            pl.BlockSpec((1, d), lambda i: (0, 0)),
            pl.BlockSpec((2, FNET_WIDTH, FNET_WIDTH), lambda i: (0, 0, 0)),
        ],
        out_specs=pl.BlockSpec((tm, 2 * d), lambda i: (i, 0)),
        out_shape=jax.ShapeDtypeStruct((t, 2 * d), BF16),
        compiler_params=_params("parallel"),
        name="fnet_chan",
    )(x, g, w_cs)


def _fnet_seq_kernel(cl_ref, sl_ref, p_ref, q_ref, o_ref):
    acc = jnp.dot(cl_ref[...], p_ref[...], preferred_element_type=F32)
    acc += jnp.dot(sl_ref[...], q_ref[...], preferred_element_type=F32)
    o_ref[...] = acc.astype(o_ref.dtype)


def fnet_seq(cl, sl_neg, pq, *, tm, tn):
    t = pq.shape[0]
    d = pq.shape[1] // 2
    nb = t // SEQ
    ni = SEQ // tm
    nj = d // tn
    return pl.pallas_call(
        _fnet_seq_kernel,
        grid=(nb, nj, ni),
        in_specs=[
            pl.BlockSpec((tm, SEQ), lambda b, j, i: (i, 0)),
            pl.BlockSpec((tm, SEQ), lambda b, j, i: (i, 0)),
            pl.BlockSpec((SEQ, tn), lambda b, j, i: (b, j)),
            pl.BlockSpec((SEQ, tn), lambda b, j, i: (b, nj + j)),
        ],
        out_specs=pl.BlockSpec((tm, tn), lambda b, j, i: (b * ni + i, j)),
        out_shape=jax.ShapeDtypeStruct((t, d), BF16),
        compiler_params=_params("parallel", "parallel", "arbitrary"),
        name="fnet_seq",
    )(cl, sl_neg, pq, pq)


def _dft_tables():
    def cos_sin(n):
        idx = jnp.arange(n, dtype=jnp.int32)
        ang = ((idx[:, None] * idx[None, :]) % n).astype(F32) * (2.0 * np.pi / n)
        return jnp.cos(ang), jnp.sin(ang)

    cc, sc = cos_sin(FNET_WIDTH)
    cl, sl = cos_sin(SEQ)
    scale = 1.0 / np.sqrt(SEQ * FNET_WIDTH)
    w_cs = jnp.stack([cc, sc]).astype(BF16)
    return w_cs, (cl * scale).astype(BF16), (-sl * scale).astype(BF16)


ATTN_PAIR = 2 * ATTN_HEAD_DIM
ATTN_KEYS = 3 * WINDOW
ATTN_Q_TILE = 2 * ATTN_GROUP * ATTN_HEAD_DIM


def _rope(v, tc, ta, tb):
    n = v.shape[-1]
    half = ROPE_DIM // 2
    return v * tc + pltpu.roll(v, n - half, 1) * ta + pltpu.roll(v, half, 1) * tb


def _attn_kernel(q_ref, k_ref, v_ref, tcq_ref, taq_ref, tbq_ref, tck_ref, tak_ref, tbk_ref, sink_ref,
                 o_ref, kdup_ref, vlo_ref, vhi_ref):
    n = pl.program_id(2)
    n_kv = 2

    @pl.when(n == 0)
    def _():
        kr = _rope(k_ref[...], tck_ref[...], tak_ref[...], tbk_ref[...])
        vv = v_ref[...]
        kr_sw = pltpu.roll(kr, ATTN_HEAD_DIM, 1)
        vv_sw = pltpu.roll(vv, ATTN_HEAD_DIM, 1)
        lo = lax.broadcasted_iota(jnp.int32, (SEQ, ATTN_PAIR), 1) < ATTN_HEAD_DIM
        kdup_ref[0] = jnp.where(lo, kr, kr_sw).astype(BF16)
        kdup_ref[1] = jnp.where(lo, kr_sw, kr).astype(BF16)
        vlo_ref[0] = jnp.where(lo, vv, 0.0).astype(BF16)
        vlo_ref[1] = jnp.where(lo, vv_sw, 0.0).astype(BF16)
        vhi_ref[0] = jnp.where(lo, 0.0, vv_sw).astype(BF16)
        vhi_ref[1] = jnp.where(lo, 0.0, vv).astype(BF16)

    start = pl.multiple_of(jnp.clip((n - 1) * WINDOW, 0, SEQ - ATTN_KEYS), WINDOW)
    keys = pl.ds(start, ATTN_KEYS)
    n_stack = 2 * (ATTN_GROUP // 2)
    qpos = n * WINDOW + lax.broadcasted_iota(jnp.int32, (n_stack, WINDOW, ATTN_KEYS), 1)
    kpos = start + lax.broadcasted_iota(jnp.int32, (n_stack, WINDOW, ATTN_KEYS), 2)
    valid = (jnp.abs(qpos - kpos) <= WINDOW).reshape(n_stack * WINDOW, ATTN_KEYS)
    q_lo = lax.broadcasted_iota(jnp.int32, (WINDOW, ATTN_PAIR), 1) < ATTN_HEAD_DIM
    q_scale = (ATTN_HEAD_DIM ** -0.5) * LOG2_E

    for kv in range(n_kv):
        pairs = [kv * (ATTN_GROUP // 2) + i for i in range(ATTN_GROUP // 2)]
        qs = [_rope(q_ref[:, p * ATTN_PAIR:(p + 1) * ATTN_PAIR], tcq_ref[...], taq_ref[...], tbq_ref[...]) * q_scale
              for p in pairs]
        lhs = jnp.concatenate([jnp.where(q_lo, q, 0.0) for q in qs] + [jnp.where(q_lo, 0.0, q) for q in qs],
                              axis=0).astype(BF16)
        sink = jnp.concatenate(
            [jnp.broadcast_to(sink_ref[2 * p + half:2 * p + half + 1, :], (WINDOW, ATTN_PAIR))
             for half in range(2) for p in pairs], axis=0) * LOG2_E
        s = lax.dot_general(lhs, kdup_ref[kv, keys, :], (((1,), (1,)), ((), ())), preferred_element_type=F32)
        s = jnp.where(valid, s, -jnp.inf)
        m = jnp.maximum(jnp.broadcast_to(jnp.max(s, axis=-1, keepdims=True), sink.shape), sink)
        p_un = jnp.exp2(s - jnp.tile(m, (1, ATTN_KEYS // ATTN_PAIR))).astype(BF16)
        denom = jnp.dot(p_un, jnp.ones((ATTN_KEYS, ATTN_PAIR), BF16), preferred_element_type=F32)
        denom = denom + jnp.exp2(sink - m)
        half_rows = (ATTN_GROUP // 2) * WINDOW
        pv_lo = jnp.dot(p_un[:half_rows], vlo_ref[kv, keys, :], preferred_element_type=F32) / denom[:half_rows]
        pv_hi = jnp.dot(p_un[half_rows:], vhi_ref[kv, keys, :], preferred_element_type=F32) / denom[half_rows:]
        out = pv_lo + pv_hi
        for i, p in enumerate(pairs):
            o_ref[:, p * ATTN_PAIR:(p + 1) * ATTN_PAIR] = out[i * WINDOW:(i + 1) * WINDOW].astype(o_ref.dtype)


def window_attention(qkv, tables, sinks):
    t = qkv.shape[0]
    nb = t // SEQ
    nblk = SEQ // WINDOW
    n_pairs = ATTN_KV_HEADS // 2
    k_col0 = ATTN_Q_HEADS * ATTN_HEAD_DIM // ATTN_PAIR
    v_col0 = k_col0 + n_pairs
    tc, ta, tb = tables
    q_tab = pl.BlockSpec((WINDOW, ATTN_PAIR), lambda b, kp, n: (n, 0))
    k_tab = pl.BlockSpec((SEQ, ATTN_PAIR), lambda b, kp, n: (0, 0))
    return pl.pallas_call(
        _attn_kernel,
        grid=(nb, n_pairs, nblk),
        in_specs=[
            pl.BlockSpec((WINDOW, ATTN_Q_TILE), lambda b, kp, n: (b * nblk + n, kp)),
            pl.BlockSpec((SEQ, ATTN_PAIR), lambda b, kp, n: (b, k_col0 + kp)),
            pl.BlockSpec((SEQ, ATTN_PAIR), lambda b, kp, n: (b, v_col0 + kp)),
            q_tab, q_tab, q_tab, k_tab, k_tab, k_tab,
            pl.BlockSpec((None, 2 * ATTN_GROUP, 1), lambda b, kp, n: (kp, 0, 0)),
        ],
        out_specs=pl.BlockSpec((WINDOW, ATTN_Q_TILE), lambda b, kp, n: (b * nblk + n, kp)),
        out_shape=jax.ShapeDtypeStruct((t, ATTN_Q_HEADS * ATTN_HEAD_DIM), BF16),
        scratch_shapes=[pltpu.VMEM((2, SEQ, ATTN_PAIR), BF16)] * 3,
        compiler_params=_params("parallel", "parallel", "arbitrary"),
        name="window_attention",
    )(qkv, qkv, qkv, tc, ta, tb, tc, ta, tb, sinks.reshape(n_pairs, 2 * ATTN_GROUP, 1))


def _rope_lane_tables():
    half = ROPE_DIM // 2
    inv = 1.0 / (ROPE_THETA ** (jnp.arange(0, ROPE_DIM, 2, dtype=F32) / ROPE_DIM))
    ang = jnp.arange(SEQ, dtype=F32)[:, None] * inv[None, :]
    cos, sin = jnp.cos(ang), jnp.sin(ang)
    pad = ATTN_HEAD_DIM - ROPE_DIM
    tc = jnp.concatenate([cos, cos, jnp.ones((SEQ, pad), F32)], axis=1)
    ta = jnp.concatenate([-sin, jnp.zeros((SEQ, half + pad), F32)], axis=1)
    tb = jnp.concatenate([jnp.zeros((SEQ, half), F32), sin, jnp.zeros((SEQ, pad), F32)], axis=1)
    return tuple(jnp.tile(tab, (1, 2)) for tab in (tc, ta, tb))


ROW_TILE = 1024
FFN_TILE = 512


def _ssd_layer(x, g_pre, g_post, w_in, j, conv_w, conv_b, dt_bias, a_log, d_skip, norm_w, w_out):
    proj, dt_raw = norm_matmul(x, g_pre, w_in, j, tm=ROW_TILE, tn=1024, n_main=SSD_MAIN_DIM, out_dtype=BF16)
    small = _ssd_small_params(dt_bias, a_log, d_skip)
    y = ssd_scan(proj, _ssd_dt_layout(dt_raw), conv_w, conv_b[None, :], *small, norm_w[None, :])
    return matmul_norm_res(y, w_out, j, g_post, x, tm=ROW_TILE, tk=1024)


def _fnet_layer(x, g_pre, g_post, w_out, j, dft):
    w_cs, cl, sl_neg = dft
    pq = fnet_chan(x, g_pre, w_cs, tm=ROW_TILE)
    f = fnet_seq(cl, sl_neg, pq, tm=1024, tn=1024)
    return matmul_norm_res(f, w_out, j, g_post, x, tm=ROW_TILE, tk=D_MODEL)


def _attn_layer(x, g_pre, g_post, w_in, j, sinks, w_out, rope):
    qkv = norm_matmul(x, g_pre, w_in, j, tm=ROW_TILE, tn=1024, n_main=ATTN_IN_DIM, out_dtype=F32)
    a = window_attention(qkv, rope, sinks)
    return matmul_norm_res(a, w_out, j, g_post, x, tm=ROW_TILE, tk=D_MODEL)


def kernel(x_prompt, x_sample, norm_w, ffn_w_in, ffn_w_out, ssd_w_in, ssd_conv_w, ssd_conv_b, ssd_dt_bias,
           ssd_a_log, ssd_d, ssd_norm_w, ssd_w_out, fnet_w_out, attn_w_in, attn_sinks, attn_w_out):
    n_prompt = x_prompt.shape[0]
    assert x_prompt.shape[1:] == (SEQ, D_MODEL) and x_sample.shape[1:] == (SEQ, D_MODEL)
    x = jnp.concatenate([x_prompt, x_sample], axis=0).reshape(-1, D_MODEL)
    dft = _dft_tables()
    rope = _rope_lane_tables()
    ffn_w_in, ffn_w_out, ssd_w_in, ssd_w_out, fnet_w_out, attn_w_in, attn_w_out = (
        w.astype(BF16) for w in (ffn_w_in, ffn_w_out, ssd_w_in, ssd_w_out, fnet_w_out, attn_w_in, attn_w_out))
    for i in range(DEPTH):
        kind, j = i % N_MIXERS, i // N_MIXERS
        g = norm_w[i][:, None, :]
        if kind == 0:
            x = _ssd_layer(x, g[0], g[1], ssd_w_in, j, ssd_conv_w[j], ssd_conv_b[j], ssd_dt_bias[j],
                           ssd_a_log[j], ssd_d[j], ssd_norm_w[j], ssd_w_out)
        elif kind == 1:
            x = _fnet_layer(x, g[0], g[1], fnet_w_out, j, dft)
        else:
            x = _attn_layer(x, g[0], g[1], attn_w_in, j, attn_sinks[j], attn_w_out, rope)
        if i < DEPTH - 1:
            x = ffn(x, g[2], ffn_w_in, ffn_w_out, i, g[3], tm=ROW_TILE, tf=FFN_TILE)
    split = n_prompt * SEQ
    last = DEPTH - 1
    g = norm_w[last][:, None, :]
    y_prompt = ffn(x, g[2], ffn_w_in, ffn_w_out, last, g[3], tm=ROW_TILE, tf=FFN_TILE, rows=(0, split))
    y_sample = ffn(x, g[2], ffn_w_in, ffn_w_out, last, g[3], tm=ROW_TILE, tf=FFN_TILE, rows=(split, x.shape[0]))
    return (y_prompt.reshape(-1, SEQ, D_MODEL), y_sample.reshape(-1, SEQ, D_MODEL))
```

```python
import functools

import numpy as np
import jax
import jax.numpy as jnp
from jax import lax
from jax.experimental import pallas as pl
from jax.experimental.pallas import tpu as pltpu

F32 = jnp.float32
BF16 = jnp.bfloat16

D_MODEL = 2048
SEQ = 2048
DEPTH = 4
N_MIXERS = 3
NORM_EPS = 1e-6

D_INNER = 2 * D_MODEL
SSD_HEAD_DIM = 64
SSD_HEADS = D_INNER // SSD_HEAD_DIM
SSD_GROUPS = 8
SSD_HEADS_PER_GROUP = SSD_HEADS // SSD_GROUPS
SSD_STATE = 128
SSD_CONV = 5
SSD_CHUNK = 128
SSD_GROUP_WIDTH = D_INNER // SSD_GROUPS
CONV_DIM = D_INNER + 2 * SSD_GROUPS * SSD_STATE
SSD_MAIN_DIM = D_INNER + CONV_DIM

FNET_GROUPS = 4
FNET_WIDTH = D_MODEL // FNET_GROUPS

ATTN_HEAD_DIM = 64
ATTN_Q_HEADS = D_MODEL // ATTN_HEAD_DIM
ATTN_KV_HEADS = 8
ATTN_GROUP = ATTN_Q_HEADS // ATTN_KV_HEADS
WINDOW = 128
ROPE_THETA = 500000.0
ROPE_DIM = ATTN_HEAD_DIM // 4
ATTN_IN_DIM = (ATTN_Q_HEADS + 2 * ATTN_KV_HEADS) * ATTN_HEAD_DIM

D_FF = ((8 * D_MODEL + 3 * 256 - 1) // (3 * 256)) * 256

VMEM_LIMIT_BYTES = 56 * 1024 * 1024


def _params(*semantics):
    return pltpu.CompilerParams(dimension_semantics=semantics, vmem_limit_bytes=VMEM_LIMIT_BYTES)


def _rms_scale(v, g):
    ms = jnp.mean(v * v, axis=-1, keepdims=True)
    return v * lax.rsqrt(ms + NORM_EPS) * g


def _norm_matmul_kernel(x_ref, g_ref, w_ref, *rest, has_tail):
    if has_tail:
        wt_ref, o_ref, ot_ref, h_ref = rest
    else:
        o_ref, h_ref = rest

    @pl.when(pl.program_id(1) == 0)
    def _():
        h_ref[...] = _rms_scale(x_ref[...], g_ref[...]).astype(BF16)
        if has_tail:
            ot_ref[...] = jnp.dot(h_ref[...], wt_ref[...], preferred_element_type=F32)

    o_ref[...] = jnp.dot(h_ref[...], w_ref[...], preferred_element_type=F32).astype(o_ref.dtype)


def norm_matmul(x, g, w, layer, *, tm, tn, n_main, out_dtype):
    t, k = x.shape
    n_tail = w.shape[2] - n_main
    has_tail = n_tail > 0
    in_specs = [
        pl.BlockSpec((tm, k), lambda i, j: (i, 0)),
        pl.BlockSpec((1, k), lambda i, j: (0, 0)),
        pl.BlockSpec((None, k, tn), lambda i, j: (layer, 0, j)),
    ]
    out_specs = [pl.BlockSpec((tm, tn), lambda i, j: (i, j))]
    out_shape = [jax.ShapeDtypeStruct((t, n_main), out_dtype)]
    args = [x, g, w]
    if has_tail:
        assert n_main % n_tail == 0
        in_specs.append(pl.BlockSpec((None, k, n_tail), lambda i, j: (layer, 0, n_main // n_tail)))
        out_specs.append(pl.BlockSpec((tm, n_tail), lambda i, j: (i, 0)))
        out_shape.append(jax.ShapeDtypeStruct((t, n_tail), F32))
        args.append(w)
    outs = pl.pallas_call(
        functools.partial(_norm_matmul_kernel, has_tail=has_tail),
        grid=(t // tm, n_main // tn),
        in_specs=in_specs,
        out_specs=out_specs,
        out_shape=out_shape,
        scratch_shapes=[pltpu.VMEM((tm, k), BF16)],
        compiler_params=_params("parallel", "arbitrary"),
        name="norm_matmul",
    )(*args)
    return tuple(outs) if has_tail else outs[0]


def _matmul_norm_res_kernel(a_ref, w_ref, g_ref, x_ref, o_ref, *, nk):
    if nk == 1:
        o_ref[...] = jnp.dot(a_ref[...], w_ref[...], preferred_element_type=F32)
        o_ref[...] = x_ref[...] + _rms_scale(o_ref[...], g_ref[...])
        return
    k = pl.program_id(1)

    @pl.when(k == 0)
    def _():
        o_ref[...] = jnp.zeros_like(o_ref)

    o_ref[...] += jnp.dot(a_ref[...], w_ref[...], preferred_element_type=F32)

    @pl.when(k == nk - 1)
    def _():
        o_ref[...] = x_ref[...] + _rms_scale(o_ref[...], g_ref[...])


def matmul_norm_res(a, w, layer, g, x, *, tm, tk):
    t, kdim = a.shape
    d = w.shape[2]
    nk = kdim // tk
    return pl.pallas_call(
        functools.partial(_matmul_norm_res_kernel, nk=nk),
        grid=(t // tm, nk),
        in_specs=[
            pl.BlockSpec((tm, tk), lambda i, k: (i, k)),
            pl.BlockSpec((None, tk, d), lambda i, k: (layer, k, 0),
                         pipeline_mode=pl.Buffered(1) if nk == 1 else None),
            pl.BlockSpec((1, d), lambda i, k: (0, 0)),
            pl.BlockSpec((tm, d), lambda i, k: (i, 0)),
        ],
        out_specs=pl.BlockSpec((tm, d), lambda i, k: (i, 0)),
        out_shape=jax.ShapeDtypeStruct((t, d), F32),
        compiler_params=_params("parallel", "arbitrary"),
        name="matmul_norm_res",
    )(a, w, g, x)


def _ffn_kernel(x_ref, g_in_ref, w_gate_ref, w_up_ref, w_out_ref, g_out_ref, o_ref, h_ref, *, nf):
    f = pl.program_id(1)

    @pl.when(f == 0)
    def _():
        h_ref[...] = _rms_scale(x_ref[...], g_in_ref[...]).astype(BF16)
        o_ref[...] = jnp.zeros_like(o_ref)

    h = h_ref[...]
    gate = jnp.dot(h, w_gate_ref[...], preferred_element_type=F32)
    up = jnp.dot(h, w_up_ref[...], preferred_element_type=F32)
    act = (gate * jax.nn.sigmoid(gate) * up).astype(BF16)
    o_ref[...] += jnp.dot(act, w_out_ref[...], preferred_element_type=F32)

    @pl.when(f == nf - 1)
    def _():
        o_ref[...] = x_ref[...] + _rms_scale(o_ref[...], g_out_ref[...])


def ffn(x, g_in, w_in, w_out, layer, g_out, *, tm, tf, rows=None):
    d = x.shape[1]
    row0, row1 = rows if rows is not None else (0, x.shape[0])
    i0 = row0 // tm
    nf = D_FF // tf
    return pl.pallas_call(
        functools.partial(_ffn_kernel, nf=nf),
        grid=((row1 - row0) // tm, nf),
        in_specs=[
            pl.BlockSpec((tm, d), lambda i, f: (i0 + i, 0), pipeline_mode=pl.Buffered(1)),
            pl.BlockSpec((1, d), lambda i, f: (0, 0)),
            pl.BlockSpec((None, d, tf), lambda i, f: (layer, 0, f)),
            pl.BlockSpec((None, d, tf), lambda i, f: (layer, 0, nf + f)),
            pl.BlockSpec((None, tf, d), lambda i, f: (layer, f, 0)),
            pl.BlockSpec((1, d), lambda i, f: (0, 0)),
        ],
        out_specs=pl.BlockSpec((tm, d), lambda i, f: (i, 0)),
        out_shape=jax.ShapeDtypeStruct((row1 - row0, d), F32),
        scratch_shapes=[pltpu.VMEM((tm, d), BF16)],
        compiler_params=_params("parallel", "arbitrary"),
        name="ffn",
    )(x, g_in, w_in, w_in, w_out, g_out)


SSD_CONV_ROWS = 256
SSD_CONV_HALO = 16
SSD_DT_LANES = 2 * SSD_HEADS_PER_GROUP
SSD_PAIR = 2 * SSD_HEAD_DIM


def _conv_silu(src_ref, w_ref, b_ref, dst_ref):
    rows, halo = SSD_CONV_ROWS, SSD_CONV_HALO
    n_steps = SEQ // rows
    ext_rows = rows + 2 * halo
    pad = SSD_CONV // 2
    w = w_ref[...]
    bias = b_ref[...]

    def body(i, carry):
        r0 = pl.multiple_of(i * rows, rows)
        cur = src_ref[pl.ds(r0, rows), :].astype(F32)
        rp = pl.multiple_of(jnp.maximum(r0 - halo, 0), halo)
        rn = pl.multiple_of(jnp.minimum(r0 + rows, SEQ - halo), halo)
        prev = src_ref[pl.ds(rp, halo), :].astype(F32) * jnp.where(i > 0, 1.0, 0.0)
        nxt = src_ref[pl.ds(rn, halo), :].astype(F32) * jnp.where(i < n_steps - 1, 1.0, 0.0)
        ext = jnp.concatenate([prev, cur, nxt], axis=0)
        acc = cur * w[pad:pad + 1, :] + bias
        for d in range(-pad, pad + 1):
            if d == 0:
                continue
            shifted = pltpu.roll(ext, (ext_rows - d) % ext_rows, 0)[halo:halo + rows]
            acc = acc + shifted * w[pad + d:pad + d + 1, :]
        dst_ref[pl.ds(r0, rows), :] = (acc * jax.nn.sigmoid(acc)).astype(dst_ref.dtype)
        return carry

    lax.fori_loop(0, n_steps, body, 0)


def _prefix_sum(a, axis):
    n = a.shape[axis]
    idx = lax.broadcasted_iota(jnp.int32, a.shape, axis)
    s = 1
    while s < n:
        a = a + jnp.where(idx >= s, pltpu.roll(a, s, axis), 0.0)
        s *= 2
    return a


LOG2_E = float(np.log2(np.e))


def _ssd_token_tables(dtr_ref, biasr_ref, alogr_ref, xs_ref, bs_ref, cs_ref,
                      w_ref, v_ref, ds_ref, et_ref, wcol_ref, bt_ref, cb_ref, xpair_ref):
    q = SSD_CHUNK
    n_chunks = SEQ // q
    rows = n_chunks * SSD_DT_LANES
    dt = jax.nn.softplus(dtr_ref[...].reshape(rows, q) + biasr_ref[...])
    adt = dt * (-jnp.exp(alogr_ref[...]))
    pre = _prefix_sum(adt, 1)
    tot = pre[:, q - 1:q]
    row = lax.broadcasted_iota(jnp.int32, (rows, q), 0)
    backward = (row % SSD_DT_LANES) >= SSD_HEADS_PER_GROUP
    u = jnp.where(backward, tot - pre + adt, pre)
    w = u * LOG2_E
    w_ref[...] = w
    v_ref[...] = w - jnp.log2(dt)
    ds_ref[...] = dt * jnp.exp(tot - u)
    et_ref[...] = jnp.broadcast_to(jnp.exp(tot), (rows, q))

    lane_lo = lax.broadcasted_iota(jnp.int32, (q, SSD_PAIR), 1) < SSD_HEAD_DIM
    zero = jnp.zeros((q, SSD_PAIR), BF16)
    n_pairs = SSD_HEADS_PER_GROUP // 2

    def body(c, carry):
        r0 = pl.multiple_of(c * SSD_DT_LANES, SSD_DT_LANES)
        tile = jnp.concatenate([w_ref[pl.ds(r0, SSD_DT_LANES), :], jnp.zeros((q - SSD_DT_LANES, q), F32)], axis=0)
        wcol_ref[c] = tile.T
        t0 = pl.multiple_of(c * q, q)
        bc = bs_ref[pl.ds(t0, q), :]
        bt_ref[c] = bc.astype(F32).T.astype(BF16)
        cb_ref[c] = lax.dot_general(cs_ref[pl.ds(t0, q), :], bc, (((1,), (1,)), ((), ())),
                                    preferred_element_type=F32).astype(BF16)
        xs = xs_ref[pl.ds(t0, q), :].astype(BF16)
        for k in range(n_pairs):
            x2 = xs[:, k * SSD_PAIR:(k + 1) * SSD_PAIR]
            xpair_ref[c * n_pairs + k] = jnp.concatenate(
                [jnp.where(lane_lo, x2, zero), jnp.where(lane_lo, zero, x2)], axis=0)
        return carry

    lax.fori_loop(0, n_chunks, body, 0, unroll=4)


def _ssd_chunk(c, backward, refs, st_ref):
    cs_ref, v_ref, ds_ref, et_ref, wcol_ref, bt_ref, cb_ref, xpair_ref = refs
    q = SSD_CHUNK
    hpg = SSD_HEADS_PER_GROUP
    h0 = hpg * int(backward)
    t0 = pl.multiple_of(c * q, q)
    r0 = pl.multiple_of(c * SSD_DT_LANES + h0, hpg)

    v = v_ref[pl.ds(r0, hpg), :]
    dsr = ds_ref[pl.ds(r0, hpg), :]
    etot = et_ref[pl.ds(r0, hpg), :]
    w_col = wcol_ref[c]
    bt = bt_ref[c]
    cb = cb_ref[c]
    cc = cs_ref[pl.ds(t0, q), :]
    row_i = lax.broadcasted_iota(jnp.int32, (q, q), 0)
    col_j = lax.broadcasted_iota(jnp.int32, (q, q), 1)
    mask = (col_j >= row_i) if backward else (col_j <= row_i)
    lane_lo = lax.broadcasted_iota(jnp.int32, (q, SSD_PAIR), 1) < SSD_HEAD_DIM

    y_tiles = []
    for k in range(hpg // 2):
        cols = slice(k * SSD_PAIR, (k + 1) * SSD_PAIR)
        ms, bts, ods, ets = [], [], [], []
        for h in (2 * k, 2 * k + 1):
            w_i = jnp.broadcast_to(w_col[:, h0 + h:h0 + h + 1], (q, q))
            decay_dt = jnp.exp2(jnp.where(mask, w_i - v[h:h + 1, :], -jnp.inf))
            ms.append(cb * decay_dt.astype(BF16))
            bts.append(bt * jnp.broadcast_to(dsr[h:h + 1, :], (SSD_STATE, q)).astype(BF16))
            ods.append(jnp.exp2(w_i))
            ets.append(jnp.broadcast_to(etot[h:h + 1, :], (SSD_STATE, SSD_PAIR)))
        lhs = jnp.concatenate([jnp.concatenate(ms, axis=1), jnp.concatenate(bts, axis=1)], axis=0)
        res = jnp.dot(lhs, xpair_ref[c * (hpg // 2) + k], preferred_element_type=F32)
        state = st_ref[:, cols]
        y_off = jnp.dot(cc, state.astype(BF16), preferred_element_type=F32)
        y_tiles.append(res[:q] + y_off * jnp.where(lane_lo, ods[0], ods[1]))
        st_ref[:, cols] = state * jnp.where(lane_lo, ets[0], ets[1]) + res[q:]
    return jnp.concatenate(y_tiles, axis=1)


def _ssd_kernel(z_ref, x_ref, b_ref, c_ref, dtr_ref,
                cwx_ref, cwb_ref, cwc_ref, cbx_ref, cbb_ref, cbc_ref,
                biasr_ref, alogr_ref, dskip_ref, nw_ref,
                o_ref, xs_ref, bs_ref, cs_ref, yf_ref, yb_ref, stf_ref, stb_ref,
                w_ref, v_ref, ds_ref, et_ref, wcol_ref, bt_ref, cb_ref, xpair_ref):
    _conv_silu(x_ref, cwx_ref, cbx_ref, xs_ref)
    _conv_silu(b_ref, cwb_ref, cbb_ref, bs_ref)
    _conv_silu(c_ref, cwc_ref, cbc_ref, cs_ref)
    _ssd_token_tables(dtr_ref, biasr_ref, alogr_ref, xs_ref, bs_ref, cs_ref,
                      w_ref, v_ref, ds_ref, et_ref, wcol_ref, bt_ref, cb_ref, xpair_ref)
    refs = (cs_ref, v_ref, ds_ref, et_ref, wcol_ref, bt_ref, cb_ref, xpair_ref)
    n_chunks = SEQ // SSD_CHUNK
    q = SSD_CHUNK
    stf_ref[...] = jnp.zeros_like(stf_ref)
    stb_ref[...] = jnp.zeros_like(stb_ref)

    def rows(c):
        return pl.ds(pl.multiple_of(c * q, q), q)

    def finish(c, y):
        y = y + dskip_ref[...] * xs_ref[rows(c), :]
        z = z_ref[rows(c), :].astype(F32)
        y = y * (z * jax.nn.sigmoid(z))
        o_ref[rows(c), :] = _rms_scale(y, nw_ref[...]).astype(o_ref.dtype)

    def first_half(t, carry):
        cf, cbk = t, n_chunks - 1 - t
        yf_ref[rows(cf), :] = _ssd_chunk(cf, False, refs, stf_ref)
        yb_ref[rows(cbk), :] = _ssd_chunk(cbk, True, refs, stb_ref)
        return carry

    def second_half(t, carry):
        cf, cbk = t, n_chunks - 1 - t
        finish(cf, _ssd_chunk(cf, False, refs, stf_ref) + yb_ref[rows(cf), :])
        finish(cbk, _ssd_chunk(cbk, True, refs, stb_ref) + yf_ref[rows(cbk), :])
        return carry

    lax.fori_loop(0, n_chunks // 2, first_half, 0, unroll=2)
    lax.fori_loop(n_chunks // 2, n_chunks, second_half, 0, unroll=2)


def ssd_scan(proj, dt_row, conv_w, conv_b, bias_row, alog_row, d_skip, norm_w):
    t = proj.shape[0]
    nb = t // SEQ
    gw, ns = SSD_GROUP_WIDTH, SSD_STATE
    g_blocks = D_INNER // gw
    b_col0 = (2 * D_INNER) // ns
    c_col0 = b_col0 + SSD_GROUPS
    n_chunks = SEQ // SSD_CHUNK
    table_rows = n_chunks * SSD_DT_LANES

    def cw(width, col0):
        return pl.BlockSpec((SSD_CONV, width), lambda b, g: (0, col0 + g))

    def cb(width, col0):
        return pl.BlockSpec((1, width), lambda b, g: (0, col0 + g))

    return pl.pallas_call(
        _ssd_kernel,
        grid=(nb, SSD_GROUPS),
        in_specs=[
            pl.BlockSpec((SEQ, gw), lambda b, g: (b, g)),
            pl.BlockSpec((SEQ, gw), lambda b, g: (b, g_blocks + g)),
            pl.BlockSpec((SEQ, ns), lambda b, g: (b, b_col0 + g)),
            pl.BlockSpec((SEQ, ns), lambda b, g: (b, c_col0 + g)),
            pl.BlockSpec((None, None, n_chunks, SSD_DT_LANES, SSD_CHUNK), lambda b, g: (b, g, 0, 0, 0)),
            cw(gw, 0), cw(ns, D_INNER // ns), cw(ns, D_INNER // ns + SSD_GROUPS),
            cb(gw, 0), cb(ns, D_INNER // ns), cb(ns, D_INNER // ns + SSD_GROUPS),
            pl.BlockSpec((None, table_rows, 1), lambda b, g: (g, 0, 0)),
            pl.BlockSpec((None, table_rows, 1), lambda b, g: (g, 0, 0)),
            pl.BlockSpec((1, gw), lambda b, g: (0, g)),
            pl.BlockSpec((1, gw), lambda b, g: (0, g)),
        ],
        out_specs=pl.BlockSpec((SEQ, gw), lambda b, g: (b, g)),
        out_shape=jax.ShapeDtypeStruct((t, D_INNER), BF16),
        scratch_shapes=[
            pltpu.VMEM((SEQ, gw), F32),
            pltpu.VMEM((SEQ, ns), BF16),
            pltpu.VMEM((SEQ, ns), BF16),
            pltpu.VMEM((SEQ, gw), F32),
            pltpu.VMEM((SEQ, gw), F32),
            pltpu.VMEM((ns, gw), F32),
            pltpu.VMEM((ns, gw), F32),
            pltpu.VMEM((table_rows, SSD_CHUNK), F32),
            pltpu.VMEM((table_rows, SSD_CHUNK), F32),
            pltpu.VMEM((table_rows, SSD_CHUNK), F32),
            pltpu.VMEM((table_rows, SSD_CHUNK), F32),
            pltpu.VMEM((n_chunks, SSD_CHUNK, SSD_CHUNK), F32),
            pltpu.VMEM((n_chunks, ns, SSD_CHUNK), BF16),
            pltpu.VMEM((n_chunks, SSD_CHUNK, SSD_CHUNK), BF16),
            pltpu.VMEM((n_chunks * gw // SSD_PAIR, 2 * SSD_CHUNK, SSD_PAIR), BF16),
        ],
        compiler_params=_params("parallel", "parallel"),
        name="ssd_scan",
    )(proj, proj, proj, proj, dt_row, conv_w, conv_w, conv_w, conv_b, conv_b, conv_b,
      bias_row, alog_row, d_skip, norm_w)


def _ssd_small_params(dt_bias, a_log, d_skip):
    def per_group(p):
        pg = p.reshape(2, SSD_GROUPS, SSD_HEADS_PER_GROUP).transpose(1, 0, 2)
        return jnp.tile(pg.reshape(SSD_GROUPS, SSD_DT_LANES, 1), (1, SEQ // SSD_CHUNK, 1))

    return per_group(dt_bias), per_group(a_log), jnp.repeat(d_skip, SSD_HEAD_DIM)[None, :]


def _ssd_dt_layout(dt_raw):
    nb = dt_raw.shape[0] // SEQ
    n_chunks = SEQ // SSD_CHUNK
    d5 = dt_raw.reshape(nb, n_chunks, SSD_CHUNK, 2, SSD_GROUPS, SSD_HEADS_PER_GROUP)
    return d5.transpose(0, 4, 1, 3, 5, 2).reshape(nb, SSD_GROUPS, n_chunks, SSD_DT_LANES, SSD_CHUNK)


def _fnet_chan_kernel(x_ref, g_ref, w_ref, o_ref):
    h = _rms_scale(x_ref[...], g_ref[...]).astype(BF16)
    d = x_ref.shape[1]
    for gi in range(FNET_GROUPS):
        cols = slice(gi * FNET_WIDTH, (gi + 1) * FNET_WIDTH)
        for kind in range(2):
            out = jnp.dot(h[:, cols], w_ref[kind], preferred_element_type=F32)
            o_ref[:, kind * d + gi * FNET_WIDTH:kind * d + (gi + 1) * FNET_WIDTH] = out.astype(o_ref.dtype)


def fnet_chan(x, g, w_cs, *, tm):
    t, d = x.shape
    return pl.pallas_call(
        _fnet_chan_kernel,
        grid=(t // tm,),
        in_specs=[
            pl.BlockSpec((tm, d), lambda i: (i, 0)),
            pl.BlockSpec((1, d), lambda i: (0, 0)),
            pl.BlockSpec((2, FNET_WIDTH, FNET_WIDTH), lambda i: (0, 0, 0)),
        ],
        out_specs=pl.BlockSpec((tm, 2 * d), lambda i: (i, 0)),
        out_shape=jax.ShapeDtypeStruct((t, 2 * d), BF16),
        compiler_params=_params("parallel"),
        name="fnet_chan",
    )(x, g, w_cs)


def _fnet_seq_kernel(cl_ref, sl_ref, p_ref, q_ref, o_ref):
    acc = jnp.dot(cl_ref[...], p_ref[...], preferred_element_type=F32)
    acc += jnp.dot(sl_ref[...], q_ref[...], preferred_element_type=F32)
    o_ref[...] = acc.astype(o_ref.dtype)


def fnet_seq(cl, sl_neg, pq, *, tm, tn):
    t = pq.shape[0]
    d = pq.shape[1] // 2
    nb = t // SEQ
    ni = SEQ // tm
    nj = d // tn
    return pl.pallas_call(
        _fnet_seq_kernel,
        grid=(nb, nj, ni),
        in_specs=[
            pl.BlockSpec((tm, SEQ), lambda b, j, i: (i, 0)),
            pl.BlockSpec((tm, SEQ), lambda b, j, i: (i, 0)),
            pl.BlockSpec((SEQ, tn), lambda b, j, i: (b, j)),
            pl.BlockSpec((SEQ, tn), lambda b, j, i: (b, nj + j)),
        ],
        out_specs=pl.BlockSpec((tm, tn), lambda b, j, i: (b * ni + i, j)),
        out_shape=jax.ShapeDtypeStruct((t, d), BF16),
        compiler_params=_params("parallel", "parallel", "arbitrary"),
        name="fnet_seq",
    )(cl, sl_neg, pq, pq)


def _dft_tables():
    def cos_sin(n):
        idx = jnp.arange(n, dtype=jnp.int32)
        ang = ((idx[:, None] * idx[None, :]) % n).astype(F32) * (2.0 * np.pi / n)
        return jnp.cos(ang), jnp.sin(ang)

    cc, sc = cos_sin(FNET_WIDTH)
    cl, sl = cos_sin(SEQ)
    scale = 1.0 / np.sqrt(SEQ * FNET_WIDTH)
    w_cs = jnp.stack([cc, sc]).astype(BF16)
    return w_cs, (cl * scale).astype(BF16), (-sl * scale).astype(BF16)


ATTN_PAIR = 2 * ATTN_HEAD_DIM
ATTN_KEYS = 3 * WINDOW
ATTN_Q_TILE = 2 * ATTN_GROUP * ATTN_HEAD_DIM


def _rope(v, tc, ta, tb):
    n = v.shape[-1]
    half = ROPE_DIM // 2
    return v * tc + pltpu.roll(v, n - half, 1) * ta + pltpu.roll(v, half, 1) * tb


def _attn_kernel(q_ref, k_ref, v_ref, tcq_ref, taq_ref, tbq_ref, tck_ref, tak_ref, tbk_ref, sink_ref,
                 o_ref, kdup_ref, vlo_ref, vhi_ref):
    n = pl.program_id(2)
    n_kv = 2

    @pl.when(n == 0)
    def _():
        kr = _rope(k_ref[...], tck_ref[...], tak_ref[...], tbk_ref[...])
        vv = v_ref[...]
        kr_sw = pltpu.roll(kr, ATTN_HEAD_DIM, 1)
        vv_sw = pltpu.roll(vv, ATTN_HEAD_DIM, 1)
        lo = lax.broadcasted_iota(jnp.int32, (SEQ, ATTN_PAIR), 1) < ATTN_HEAD_DIM
        kdup_ref[0] = jnp.where(lo, kr, kr_sw).astype(BF16)
        kdup_ref[1] = jnp.where(lo, kr_sw, kr).astype(BF16)
        ones = jnp.ones((SEQ, ATTN_PAIR), BF16)
        vlo_ref[0] = jnp.concatenate([jnp.where(lo, vv, 0.0).astype(BF16), ones], axis=1)
        vlo_ref[1] = jnp.concatenate([jnp.where(lo, vv_sw, 0.0).astype(BF16), ones], axis=1)
        vhi_ref[0] = jnp.concatenate([jnp.where(lo, 0.0, vv_sw).astype(BF16), ones], axis=1)
        vhi_ref[1] = jnp.concatenate([jnp.where(lo, 0.0, vv).astype(BF16), ones], axis=1)

    start = pl.multiple_of(jnp.clip((n - 1) * WINDOW, 0, SEQ - ATTN_KEYS), WINDOW)
    keys = pl.ds(start, ATTN_KEYS)
    n_stack = 2 * (ATTN_GROUP // 2)
    qpos = n * WINDOW + lax.broadcasted_iota(jnp.int32, (n_stack, WINDOW, ATTN_KEYS), 1)
    kpos = start + lax.broadcasted_iota(jnp.int32, (n_stack, WINDOW, ATTN_KEYS), 2)
    valid = (jnp.abs(qpos - kpos) <= WINDOW).reshape(n_stack * WINDOW, ATTN_KEYS)
    q_lo = lax.broadcasted_iota(jnp.int32, (WINDOW, ATTN_PAIR), 1) < ATTN_HEAD_DIM
    q_scale = (ATTN_HEAD_DIM ** -0.5) * LOG2_E

    half_rows = (ATTN_GROUP // 2) * WINDOW
    pairs_of = [[kv * (ATTN_GROUP // 2) + i for i in range(ATTN_GROUP // 2)] for kv in range(n_kv)]
    scores, sinks = [], []
    for kv in range(n_kv):
        pairs = pairs_of[kv]
        qs = [_rope(q_ref[:, p * ATTN_PAIR:(p + 1) * ATTN_PAIR], tcq_ref[...], taq_ref[...], tbq_ref[...]) * q_scale
              for p in pairs]
        lhs = jnp.concatenate([jnp.where(q_lo, q, 0.0) for q in qs] + [jnp.where(q_lo, 0.0, q) for q in qs],
                              axis=0).astype(BF16)
        sinks.append(jnp.concatenate(
            [jnp.broadcast_to(sink_ref[2 * p + half:2 * p + half + 1, :], (WINDOW, ATTN_PAIR))
             for half in range(2) for p in pairs], axis=0) * LOG2_E)
        s = lax.dot_general(lhs, kdup_ref[kv, keys, :], (((1,), (1,)), ((), ())), preferred_element_type=F32)
        scores.append(jnp.where(valid, s, -jnp.inf))
    probs, e_sinks = [], []
    for kv in range(n_kv):
        s, sink = scores[kv], sinks[kv]
        m = jnp.maximum(jnp.broadcast_to(jnp.max(s, axis=-1, keepdims=True), sink.shape), sink)
        probs.append(jnp.exp2(s - jnp.tile(m, (1, ATTN_KEYS // ATTN_PAIR))).astype(BF16))
        e_sinks.append(jnp.exp2(sink - m))
    for kv in range(n_kv):
        p_un, e_sink = probs[kv], e_sinks[kv]
        pv_lo = jnp.dot(p_un[:half_rows], vlo_ref[kv, keys, :], preferred_element_type=F32)
        pv_hi = jnp.dot(p_un[half_rows:], vhi_ref[kv, keys, :], preferred_element_type=F32)
        out = (pv_lo[:, :ATTN_PAIR] / (pv_lo[:, ATTN_PAIR:] + e_sink[:half_rows])
               + pv_hi[:, :ATTN_PAIR] / (pv_hi[:, ATTN_PAIR:] + e_sink[half_rows:]))
        for i, p in enumerate(pairs_of[kv]):
            o_ref[:, p * ATTN_PAIR:(p + 1) * ATTN_PAIR] = out[i * WINDOW:(i + 1) * WINDOW].astype(o_ref.dtype)


def window_attention(qkv, tables, sinks):
    t = qkv.shape[0]
    nb = t // SEQ
    nblk = SEQ // WINDOW
    n_pairs = ATTN_KV_HEADS // 2
    k_col0 = ATTN_Q_HEADS * ATTN_HEAD_DIM // ATTN_PAIR
    v_col0 = k_col0 + n_pairs
    tc, ta, tb = tables
    q_tab = pl.BlockSpec((WINDOW, ATTN_PAIR), lambda b, kp, n: (n, 0))
    k_tab = pl.BlockSpec((SEQ, ATTN_PAIR), lambda b, kp, n: (0, 0))
    return pl.pallas_call(
        _attn_kernel,
        grid=(nb, n_pairs, nblk),
        in_specs=[
            pl.BlockSpec((WINDOW, ATTN_Q_TILE), lambda b, kp, n: (b * nblk + n, kp)),
            pl.BlockSpec((SEQ, ATTN_PAIR), lambda b, kp, n: (b, k_col0 + kp)),
            pl.BlockSpec((SEQ, ATTN_PAIR), lambda b, kp, n: (b, v_col0 + kp)),
            q_tab, q_tab, q_tab, k_tab, k_tab, k_tab,
            pl.BlockSpec((None, 2 * ATTN_GROUP, 1), lambda b, kp, n: (kp, 0, 0)),
        ],
        out_specs=pl.BlockSpec((WINDOW, ATTN_Q_TILE), lambda b, kp, n: (b * nblk + n, kp)),
        out_shape=jax.ShapeDtypeStruct((t, ATTN_Q_HEADS * ATTN_HEAD_DIM), BF16),
        scratch_shapes=[pltpu.VMEM((2, SEQ, ATTN_PAIR), BF16)] + [pltpu.VMEM((2, SEQ, 2 * ATTN_PAIR), BF16)] * 2,
        compiler_params=_params("parallel", "parallel", "arbitrary"),
        name="window_attention",
    )(qkv, qkv, qkv, tc, ta, tb, tc, ta, tb, sinks.reshape(n_pairs, 2 * ATTN_GROUP, 1))


def _rope_lane_tables():
    half = ROPE_DIM // 2
    inv = 1.0 / (ROPE_THETA ** (jnp.arange(0, ROPE_DIM, 2, dtype=F32) / ROPE_DIM))
    ang = jnp.arange(SEQ, dtype=F32)[:, None] * inv[None, :]
    cos, sin = jnp.cos(ang), jnp.sin(ang)
    pad = ATTN_HEAD_DIM - ROPE_DIM
    tc = jnp.concatenate([cos, cos, jnp.ones((SEQ, pad), F32)], axis=1)
    ta = jnp.concatenate([-sin, jnp.zeros((SEQ, half + pad), F32)], axis=1)
    tb = jnp.concatenate([jnp.zeros((SEQ, half), F32), sin, jnp.zeros((SEQ, pad), F32)], axis=1)
    return tuple(jnp.tile(tab, (1, 2)) for tab in (tc, ta, tb))


ROW_TILE = 1024
FFN_TILE = 512


def _ssd_layer(x, g_pre, g_post, w_in, j, conv_w, conv_b, dt_bias, a_log, d_skip, norm_w, w_out):
    proj, dt_raw = norm_matmul(x, g_pre, w_in, j, tm=ROW_TILE, tn=1024, n_main=SSD_MAIN_DIM, out_dtype=BF16)
    small = _ssd_small_params(dt_bias, a_log, d_skip)
    y = ssd_scan(proj, _ssd_dt_layout(dt_raw), conv_w, conv_b[None, :], *small, norm_w[None, :])
    return matmul_norm_res(y, w_out, j, g_post, x, tm=ROW_TILE, tk=1024)


def _fnet_layer(x, g_pre, g_post, w_out, j, dft):
    w_cs, cl, sl_neg = dft
    pq = fnet_chan(x, g_pre, w_cs, tm=ROW_TILE)
    f = fnet_seq(cl, sl_neg, pq, tm=1024, tn=1024)
    return matmul_norm_res(f, w_out, j, g_post, x, tm=ROW_TILE, tk=D_MODEL)


def _attn_layer(x, g_pre, g_post, w_in, j, sinks, w_out, rope):
    qkv = norm_matmul(x, g_pre, w_in, j, tm=ROW_TILE, tn=1024, n_main=ATTN_IN_DIM, out_dtype=F32)
    a = window_attention(qkv, rope, sinks)
    return matmul_norm_res(a, w_out, j, g_post, x, tm=ROW_TILE, tk=D_MODEL)


def kernel(x_prompt, x_sample, norm_w, ffn_w_in, ffn_w_out, ssd_w_in, ssd_conv_w, ssd_conv_b, ssd_dt_bias,
           ssd_a_log, ssd_d, ssd_norm_w, ssd_w_out, fnet_w_out, attn_w_in, attn_sinks, attn_w_out):
    n_prompt = x_prompt.shape[0]
    assert x_prompt.shape[1:] == (SEQ, D_MODEL) and x_sample.shape[1:] == (SEQ, D_MODEL)
    x = jnp.concatenate([x_prompt, x_sample], axis=0).reshape(-1, D_MODEL)
    dft = _dft_tables()
    rope = _rope_lane_tables()
    ffn_w_in, ffn_w_out, ssd_w_in, ssd_w_out, fnet_w_out, attn_w_in, attn_w_out = (
        w.astype(BF16) for w in (ffn_w_in, ffn_w_out, ssd_w_in, ssd_w_out, fnet_w_out, attn_w_in, attn_w_out))
    for i in range(DEPTH):
        kind, j = i % N_MIXERS, i // N_MIXERS
        g = norm_w[i][:, None, :]
        if kind == 0:
            x = _ssd_layer(x, g[0], g[1], ssd_w_in, j, ssd_conv_w[j], ssd_conv_b[j], ssd_dt_bias[j],
                           ssd_a_log[j], ssd_d[j], ssd_norm_w[j], ssd_w_out)
        elif kind == 1:
            x = _fnet_layer(x, g[0], g[1], fnet_w_out, j, dft)
        else:
            x = _attn_layer(x, g[0], g[1], attn_w_in, j, attn_sinks[j], attn_w_out, rope)
        if i < DEPTH - 1:
            x = ffn(x, g[2], ffn_w_in, ffn_w_out, i, g[3], tm=ROW_TILE, tf=FFN_TILE)
    split = n_prompt * SEQ
    last = DEPTH - 1
    g = norm_w[last][:, None, :]
    y_prompt = ffn(x, g[2], ffn_w_in, ffn_w_out, last, g[3], tm=ROW_TILE, tf=FFN_TILE, rows=(0, split))
    y_sample = ffn(x, g[2], ffn_w_in, ffn_w_out, last, g[3], tm=ROW_TILE, tf=FFN_TILE, rows=(split, x.shape[0]))
    return (y_prompt.reshape(-1, SEQ, D_MODEL), y_sample.reshape(-1, SEQ, D_MODEL))
```

```python
import functools

import numpy as np
import jax
import jax.numpy as jnp
from jax import lax
from jax.experimental import pallas as pl
from jax.experimental.pallas import tpu as pltpu

F32 = jnp.float32
BF16 = jnp.bfloat16

D_MODEL = 2048
SEQ = 2048
DEPTH = 4
N_MIXERS = 3
NORM_EPS = 1e-6

D_INNER = 2 * D_MODEL
SSD_HEAD_DIM = 64
SSD_HEADS = D_INNER // SSD_HEAD_DIM
SSD_GROUPS = 8
SSD_HEADS_PER_GROUP = SSD_HEADS // SSD_GROUPS
SSD_STATE = 128
SSD_CONV = 5
SSD_CHUNK = 128
SSD_GROUP_WIDTH = D_INNER // SSD_GROUPS
CONV_DIM = D_INNER + 2 * SSD_GROUPS * SSD_STATE
SSD_MAIN_DIM = D_INNER + CONV_DIM

FNET_GROUPS = 4
FNET_WIDTH = D_MODEL // FNET_GROUPS

ATTN_HEAD_DIM = 64
ATTN_Q_HEADS = D_MODEL // ATTN_HEAD_DIM
ATTN_KV_HEADS = 8
ATTN_GROUP = ATTN_Q_HEADS // ATTN_KV_HEADS
WINDOW = 128
ROPE_THETA = 500000.0
ROPE_DIM = ATTN_HEAD_DIM // 4
ATTN_IN_DIM = (ATTN_Q_HEADS + 2 * ATTN_KV_HEADS) * ATTN_HEAD_DIM

D_FF = ((8 * D_MODEL + 3 * 256 - 1) // (3 * 256)) * 256

VMEM_LIMIT_BYTES = 56 * 1024 * 1024


def _params(*semantics):
    return pltpu.CompilerParams(dimension_semantics=semantics, vmem_limit_bytes=VMEM_LIMIT_BYTES)


def _rms_scale(v, g):
    ms = jnp.mean(v * v, axis=-1, keepdims=True)
    return v * lax.rsqrt(ms + NORM_EPS) * g


def _norm_matmul_kernel(x_ref, g_ref, w_ref, *rest, has_tail):
    if has_tail:
        wt_ref, o_ref, ot_ref, h_ref = rest
    else:
        o_ref, h_ref = rest

    @pl.when(pl.program_id(1) == 0)
    def _():
        h_ref[...] = _rms_scale(x_ref[...], g_ref[...]).astype(BF16)
        if has_tail:
            ot_ref[...] = jnp.dot(h_ref[...], wt_ref[...], preferred_element_type=F32)

    o_ref[...] = jnp.dot(h_ref[...], w_ref[...], preferred_element_type=F32).astype(o_ref.dtype)


def norm_matmul(x, g, w, layer, *, tm, tn, n_main, out_dtype):
    t, k = x.shape
    n_tail = w.shape[2] - n_main
    has_tail = n_tail > 0
    in_specs = [
        pl.BlockSpec((tm, k), lambda i, j: (i, 0)),
        pl.BlockSpec((1, k), lambda i, j: (0, 0)),
        pl.BlockSpec((None, k, tn), lambda i, j: (layer, 0, j)),
    ]
    out_specs = [pl.BlockSpec((tm, tn), lambda i, j: (i, j))]
    out_shape = [jax.ShapeDtypeStruct((t, n_main), out_dtype)]
    args = [x, g, w]
    if has_tail:
        assert n_main % n_tail == 0
        in_specs.append(pl.BlockSpec((None, k, n_tail), lambda i, j: (layer, 0, n_main // n_tail)))
        out_specs.append(pl.BlockSpec((tm, n_tail), lambda i, j: (i, 0)))
        out_shape.append(jax.ShapeDtypeStruct((t, n_tail), F32))
        args.append(w)
    outs = pl.pallas_call(
        functools.partial(_norm_matmul_kernel, has_tail=has_tail),
        grid=(t // tm, n_main // tn),
        in_specs=in_specs,
        out_specs=out_specs,
        out_shape=out_shape,
        scratch_shapes=[pltpu.VMEM((tm, k), BF16)],
        compiler_params=_params("parallel", "arbitrary"),
        name="norm_matmul",
    )(*args)
    return tuple(outs) if has_tail else outs[0]


def _matmul_norm_res_kernel(a_ref, w_ref, g_ref, x_ref, o_ref, *, nk):
    if nk == 1:
        o_ref[...] = jnp.dot(a_ref[...], w_ref[...], preferred_element_type=F32)
        o_ref[...] = x_ref[...] + _rms_scale(o_ref[...], g_ref[...])
        return
    k = pl.program_id(1)

    @pl.when(k == 0)
    def _():
        o_ref[...] = jnp.zeros_like(o_ref)

    o_ref[...] += jnp.dot(a_ref[...], w_ref[...], preferred_element_type=F32)

    @pl.when(k == nk - 1)
    def _():
        o_ref[...] = x_ref[...] + _rms_scale(o_ref[...], g_ref[...])


def matmul_norm_res(a, w, layer, g, x, *, tm, tk):
    t, kdim = a.shape
    d = w.shape[2]
    nk = kdim // tk
    return pl.pallas_call(
        functools.partial(_matmul_norm_res_kernel, nk=nk),
        grid=(t // tm, nk),
        in_specs=[
            pl.BlockSpec((tm, tk), lambda i, k: (i, k)),
            pl.BlockSpec((None, tk, d), lambda i, k: (layer, k, 0),
                         pipeline_mode=pl.Buffered(1) if nk == 1 else None),
            pl.BlockSpec((1, d), lambda i, k: (0, 0)),
            pl.BlockSpec((tm, d), lambda i, k: (i, 0)),
        ],
        out_specs=pl.BlockSpec((tm, d), lambda i, k: (i, 0)),
        out_shape=jax.ShapeDtypeStruct((t, d), F32),
        compiler_params=_params("parallel", "arbitrary"),
        name="matmul_norm_res",
    )(a, w, g, x)


def _ffn_kernel(x_ref, g_in_ref, w_gate_ref, w_up_ref, w_out_ref, g_out_ref, o_ref, h_ref, *, nf):
    f = pl.program_id(1)

    @pl.when(f == 0)
    def _():
        h_ref[...] = _rms_scale(x_ref[...], g_in_ref[...]).astype(BF16)
        o_ref[...] = jnp.zeros_like(o_ref)

    h = h_ref[...]
    gate = jnp.dot(h, w_gate_ref[...], preferred_element_type=F32)
    up = jnp.dot(h, w_up_ref[...], preferred_element_type=F32)
    act = (gate * jax.nn.sigmoid(gate) * up).astype(BF16)
    o_ref[...] += jnp.dot(act, w_out_ref[...], preferred_element_type=F32)

    @pl.when(f == nf - 1)
    def _():
        o_ref[...] = x_ref[...] + _rms_scale(o_ref[...], g_out_ref[...])


def ffn(x, g_in, w_in, w_out, layer, g_out, *, tm, tf, rows=None):
    d = x.shape[1]
    row0, row1 = rows if rows is not None else (0, x.shape[0])
    i0 = row0 // tm
    nf = D_FF // tf
    return pl.pallas_call(
        functools.partial(_ffn_kernel, nf=nf),
        grid=((row1 - row0) // tm, nf),
        in_specs=[
            pl.BlockSpec((tm, d), lambda i, f: (i0 + i, 0), pipeline_mode=pl.Buffered(1)),
            pl.BlockSpec((1, d), lambda i, f: (0, 0)),
            pl.BlockSpec((None, d, tf), lambda i, f: (layer, 0, f)),
            pl.BlockSpec((None, d, tf), lambda i, f: (layer, 0, nf + f)),
            pl.BlockSpec((None, tf, d), lambda i, f: (layer, f, 0)),
            pl.BlockSpec((1, d), lambda i, f: (0, 0)),
        ],
        out_specs=pl.BlockSpec((tm, d), lambda i, f: (i, 0)),
        out_shape=jax.ShapeDtypeStruct((row1 - row0, d), F32),
        scratch_shapes=[pltpu.VMEM((tm, d), BF16)],
        compiler_params=_params("parallel", "arbitrary"),
        name="ffn",
    )(x, g_in, w_in, w_in, w_out, g_out)


SSD_CONV_ROWS = 256
SSD_CONV_HALO = 16
SSD_DT_LANES = 2 * SSD_HEADS_PER_GROUP
SSD_PAIR = 2 * SSD_HEAD_DIM


def _conv_silu(src_ref, w_ref, b_ref, dst_ref):
    rows, halo = SSD_CONV_ROWS, SSD_CONV_HALO
    n_steps = SEQ // rows
    ext_rows = rows + 2 * halo
    pad = SSD_CONV // 2
    w = w_ref[...]
    bias = b_ref[...]

    def body(i, carry):
        r0 = pl.multiple_of(i * rows, rows)
        cur = src_ref[pl.ds(r0, rows), :].astype(F32)
        rp = pl.multiple_of(jnp.maximum(r0 - halo, 0), halo)
        rn = pl.multiple_of(jnp.minimum(r0 + rows, SEQ - halo), halo)
        prev = src_ref[pl.ds(rp, halo), :].astype(F32) * jnp.where(i > 0, 1.0, 0.0)
        nxt = src_ref[pl.ds(rn, halo), :].astype(F32) * jnp.where(i < n_steps - 1, 1.0, 0.0)
        ext = jnp.concatenate([prev, cur, nxt], axis=0)
        acc = cur * w[pad:pad + 1, :] + bias
        for d in range(-pad, pad + 1):
            if d == 0:
                continue
            shifted = pltpu.roll(ext, (ext_rows - d) % ext_rows, 0)[halo:halo + rows]
            acc = acc + shifted * w[pad + d:pad + d + 1, :]
        dst_ref[pl.ds(r0, rows), :] = (acc * jax.nn.sigmoid(acc)).astype(dst_ref.dtype)
        return carry

    lax.fori_loop(0, n_steps, body, 0)


def _prefix_sum(a, axis):
    n = a.shape[axis]
    idx = lax.broadcasted_iota(jnp.int32, a.shape, axis)
    s = 1
    while s < n:
        a = a + jnp.where(idx >= s, pltpu.roll(a, s, axis), 0.0)
        s *= 2
    return a


LOG2_E = float(np.log2(np.e))


def _ssd_token_tables(dtr_ref, biasr_ref, alogr_ref, xs_ref, bs_ref, cs_ref,
                      w_ref, v_ref, ds_ref, et_ref, wcol_ref, bt_ref, cb_ref, xpair_ref):
    q = SSD_CHUNK
    n_chunks = SEQ // q
    rows = n_chunks * SSD_DT_LANES
    dt = jax.nn.softplus(dtr_ref[...].reshape(rows, q) + biasr_ref[...])
    adt = dt * (-jnp.exp(alogr_ref[...]))
    pre = _prefix_sum(adt, 1)
    tot = pre[:, q - 1:q]
    row = lax.broadcasted_iota(jnp.int32, (rows, q), 0)
    backward = (row % SSD_DT_LANES) >= SSD_HEADS_PER_GROUP
    u = jnp.where(backward, tot - pre + adt, pre)
    w = u * LOG2_E
    w_ref[...] = w
    v_ref[...] = w - jnp.log2(dt)
    ds_ref[...] = dt * jnp.exp(tot - u)
    et_ref[...] = jnp.broadcast_to(jnp.exp(tot), (rows, q))

    lane_lo = lax.broadcasted_iota(jnp.int32, (q, SSD_PAIR), 1) < SSD_HEAD_DIM
    zero = jnp.zeros((q, SSD_PAIR), BF16)
    n_pairs = SSD_HEADS_PER_GROUP // 2

    def body(c, carry):
        r0 = pl.multiple_of(c * SSD_DT_LANES, SSD_DT_LANES)
        tile = jnp.concatenate([w_ref[pl.ds(r0, SSD_DT_LANES), :], jnp.zeros((q - SSD_DT_LANES, q), F32)], axis=0)
        wcol_ref[c] = tile.T
        t0 = pl.multiple_of(c * q, q)
        bc = bs_ref[pl.ds(t0, q), :]
        bt_ref[c] = bc.astype(F32).T.astype(BF16)
        cb_ref[c] = lax.dot_general(cs_ref[pl.ds(t0, q), :], bc, (((1,), (1,)), ((), ())),
                                    preferred_element_type=F32).astype(BF16)
        xs = xs_ref[pl.ds(t0, q), :].astype(BF16)
        for k in range(n_pairs):
            x2 = xs[:, k * SSD_PAIR:(k + 1) * SSD_PAIR]
            xpair_ref[c * n_pairs + k] = jnp.concatenate(
                [jnp.where(lane_lo, x2, zero), jnp.where(lane_lo, zero, x2)], axis=0)
        return carry

    lax.fori_loop(0, n_chunks, body, 0, unroll=4)


def _ssd_chunk(c, backward, refs, st_ref):
    cs_ref, v_ref, ds_ref, et_ref, wcol_ref, bt_ref, cb_ref, xpair_ref = refs
    q = SSD_CHUNK
    hpg = SSD_HEADS_PER_GROUP
    h0 = hpg * int(backward)
    t0 = pl.multiple_of(c * q, q)
    r0 = pl.multiple_of(c * SSD_DT_LANES + h0, hpg)

    v = v_ref[pl.ds(r0, hpg), :]
    dsr = ds_ref[pl.ds(r0, hpg), :]
    etot = et_ref[pl.ds(r0, hpg), :]
    w_col = wcol_ref[c]
    bt = bt_ref[c]
    cb = cb_ref[c]
    cc = cs_ref[pl.ds(t0, q), :]
    row_i = lax.broadcasted_iota(jnp.int32, (q, q), 0)
    col_j = lax.broadcasted_iota(jnp.int32, (q, q), 1)
    mask = (col_j >= row_i) if backward else (col_j <= row_i)
    lane_lo = lax.broadcasted_iota(jnp.int32, (q, SSD_PAIR), 1) < SSD_HEAD_DIM

    y_tiles = []
    for k in range(hpg // 2):
        cols = slice(k * SSD_PAIR, (k + 1) * SSD_PAIR)
        ms, bts, ods, ets = [], [], [], []
        for h in (2 * k, 2 * k + 1):
            w_i = jnp.broadcast_to(w_col[:, h0 + h:h0 + h + 1], (q, q))
            decay_dt = jnp.exp2(jnp.where(mask, w_i - v[h:h + 1, :], -jnp.inf))
            ms.append(cb * decay_dt.astype(BF16))
            bts.append(bt * jnp.broadcast_to(dsr[h:h + 1, :], (SSD_STATE, q)).astype(BF16))
            ods.append(jnp.exp2(w_i))
            ets.append(jnp.broadcast_to(etot[h:h + 1, :], (SSD_STATE, SSD_PAIR)))
        lhs = jnp.concatenate([jnp.concatenate(ms, axis=1), jnp.concatenate(bts, axis=1)], axis=0)
        res = jnp.dot(lhs, xpair_ref[c * (hpg // 2) + k], preferred_element_type=F32)
        state = st_ref[:, cols]
        y_off = jnp.dot(cc, state.astype(BF16), preferred_element_type=F32)
        y_tiles.append(res[:q] + y_off * jnp.where(lane_lo, ods[0], ods[1]))
        st_ref[:, cols] = state * jnp.where(lane_lo, ets[0], ets[1]) + res[q:]
    return jnp.concatenate(y_tiles, axis=1)


def _ssd_kernel(z_ref, x_ref, b_ref, c_ref, dtr_ref,
                cwx_ref, cwb_ref, cwc_ref, cbx_ref, cbb_ref, cbc_ref,
                biasr_ref, alogr_ref, dskip_ref, nw_ref,
                o_ref, xs_ref, bs_ref, cs_ref, yf_ref, yb_ref, stf_ref, stb_ref,
                w_ref, v_ref, ds_ref, et_ref, wcol_ref, bt_ref, cb_ref, xpair_ref):
    _conv_silu(x_ref, cwx_ref, cbx_ref, xs_ref)
    _conv_silu(b_ref, cwb_ref, cbb_ref, bs_ref)
    _conv_silu(c_ref, cwc_ref, cbc_ref, cs_ref)
    _ssd_token_tables(dtr_ref, biasr_ref, alogr_ref, xs_ref, bs_ref, cs_ref,
                      w_ref, v_ref, ds_ref, et_ref, wcol_ref, bt_ref, cb_ref, xpair_ref)
    refs = (cs_ref, v_ref, ds_ref, et_ref, wcol_ref, bt_ref, cb_ref, xpair_ref)
    n_chunks = SEQ // SSD_CHUNK
    q = SSD_CHUNK
    stf_ref[...] = jnp.zeros_like(stf_ref)
    stb_ref[...] = jnp.zeros_like(stb_ref)

    def rows(c):
        return pl.ds(pl.multiple_of(c * q, q), q)

    def finish(c, y):
        y = y + dskip_ref[...] * xs_ref[rows(c), :]
        z = z_ref[rows(c), :].astype(F32)
        y = y * (z * jax.nn.sigmoid(z))
        o_ref[rows(c), :] = _rms_scale(y, nw_ref[...]).astype(o_ref.dtype)

    def first_half(t, carry):
        cf, cbk = t, n_chunks - 1 - t
        yf_ref[rows(cf), :] = _ssd_chunk(cf, False, refs, stf_ref)
        yb_ref[rows(cbk), :] = _ssd_chunk(cbk, True, refs, stb_ref)
        return carry

    def second_half(t, carry):
        cf, cbk = t, n_chunks - 1 - t
        finish(cf, _ssd_chunk(cf, False, refs, stf_ref) + yb_ref[rows(cf), :])
        finish(cbk, _ssd_chunk(cbk, True, refs, stb_ref) + yf_ref[rows(cbk), :])
        return carry

    lax.fori_loop(0, n_chunks // 2, first_half, 0, unroll=4)
    lax.fori_loop(n_chunks // 2, n_chunks, second_half, 0, unroll=4)


def ssd_scan(proj, dt_row, conv_w, conv_b, bias_row, alog_row, d_skip, norm_w):
    t = proj.shape[0]
    nb = t // SEQ
    gw, ns = SSD_GROUP_WIDTH, SSD_STATE
    g_blocks = D_INNER // gw
    b_col0 = (2 * D_INNER) // ns
    c_col0 = b_col0 + SSD_GROUPS
    n_chunks = SEQ // SSD_CHUNK
    table_rows = n_chunks * SSD_DT_LANES

    def cw(width, col0):
        return pl.BlockSpec((SSD_CONV, width), lambda b, g: (0, col0 + g))

    def cb(width, col0):
        return pl.BlockSpec((1, width), lambda b, g: (0, col0 + g))

    return pl.pallas_call(
        _ssd_kernel,
        grid=(nb, SSD_GROUPS),
        in_specs=[
            pl.BlockSpec((SEQ, gw), lambda b, g: (b, g)),
            pl.BlockSpec((SEQ, gw), lambda b, g: (b, g_blocks + g)),
            pl.BlockSpec((SEQ, ns), lambda b, g: (b, b_col0 + g)),
            pl.BlockSpec((SEQ, ns), lambda b, g: (b, c_col0 + g)),
            pl.BlockSpec((None, None, n_chunks, SSD_DT_LANES, SSD_CHUNK), lambda b, g: (b, g, 0, 0, 0)),
            cw(gw, 0), cw(ns, D_INNER // ns), cw(ns, D_INNER // ns + SSD_GROUPS),
            cb(gw, 0), cb(ns, D_INNER // ns), cb(ns, D_INNER // ns + SSD_GROUPS),
            pl.BlockSpec((None, table_rows, 1), lambda b, g: (g, 0, 0)),
            pl.BlockSpec((None, table_rows, 1), lambda b, g: (g, 0, 0)),
            pl.BlockSpec((1, gw), lambda b, g: (0, g)),
            pl.BlockSpec((1, gw), lambda b, g: (0, g)),
        ],
        out_specs=pl.BlockSpec((SEQ, gw), lambda b, g: (b, g)),
        out_shape=jax.ShapeDtypeStruct((t, D_INNER), BF16),
        scratch_shapes=[
            pltpu.VMEM((SEQ, gw), F32),
            pltpu.VMEM((SEQ, ns), BF16),
            pltpu.VMEM((SEQ, ns), BF16),
            pltpu.VMEM((SEQ, gw), F32),
            pltpu.VMEM((SEQ, gw), F32),
            pltpu.VMEM((ns, gw), F32),
            pltpu.VMEM((ns, gw), F32),
            pltpu.VMEM((table_rows, SSD_CHUNK), F32),
            pltpu.VMEM((table_rows, SSD_CHUNK), F32),
            pltpu.VMEM((table_rows, SSD_CHUNK), F32),
            pltpu.VMEM((table_rows, SSD_CHUNK), F32),
            pltpu.VMEM((n_chunks, SSD_CHUNK, SSD_CHUNK), F32),
            pltpu.VMEM((n_chunks, ns, SSD_CHUNK), BF16),
            pltpu.VMEM((n_chunks, SSD_CHUNK, SSD_CHUNK), BF16),
            pltpu.VMEM((n_chunks * gw // SSD_PAIR, 2 * SSD_CHUNK, SSD_PAIR), BF16),
        ],
        compiler_params=_params("parallel", "parallel"),
        name="ssd_scan",
    )(proj, proj, proj, proj, dt_row, conv_w, conv_w, conv_w, conv_b, conv_b, conv_b,
      bias_row, alog_row, d_skip, norm_w)


def _ssd_small_params(dt_bias, a_log, d_skip):
    def per_group(p):
        pg = p.reshape(2, SSD_GROUPS, SSD_HEADS_PER_GROUP).transpose(1, 0, 2)
        return jnp.tile(pg.reshape(SSD_GROUPS, SSD_DT_LANES, 1), (1, SEQ // SSD_CHUNK, 1))

    return per_group(dt_bias), per_group(a_log), jnp.repeat(d_skip, SSD_HEAD_DIM)[None, :]


def _ssd_dt_layout(dt_raw):
    nb = dt_raw.shape[0] // SEQ
    n_chunks = SEQ // SSD_CHUNK
    d5 = dt_raw.reshape(nb, n_chunks, SSD_CHUNK, 2, SSD_GROUPS, SSD_HEADS_PER_GROUP)
    return d5.transpose(0, 4, 1, 3, 5, 2).reshape(nb, SSD_GROUPS, n_chunks, SSD_DT_LANES, SSD_CHUNK)


def _fnet_chan_kernel(x_ref, g_ref, w_ref, o_ref):
    h = _rms_scale(x_ref[...], g_ref[...]).astype(BF16)
    d = x_ref.shape[1]
    for gi in range(FNET_GROUPS):
        cols = slice(gi * FNET_WIDTH, (gi + 1) * FNET_WIDTH)
        for kind in range(2):
            out = jnp.dot(h[:, cols], w_ref[kind], preferred_element_type=F32)
            o_ref[:, kind * d + gi * FNET_WIDTH:kind * d + (gi + 1) * FNET_WIDTH] = out.astype(o_ref.dtype)


def fnet_chan(x, g, w_cs, *, tm):
    t, d = x.shape
    return pl.pallas_call(
        _fnet_chan_kernel,
        grid=(t // tm,),
        in_specs=[
            pl.BlockSpec((tm, d), lambda i: (i, 0)),
            pl.BlockSpec((1, d), lambda i: (0, 0)),
            pl.BlockSpec((2, FNET_WIDTH, FNET_WIDTH), lambda i: (0, 0, 0)),
        ],
        out_specs=pl.BlockSpec((tm, 2 * d), lambda i: (i, 0)),
        out_shape=jax.ShapeDtypeStruct((t, 2 * d), BF16),
        compiler_params=_params("parallel"),
        name="fnet_chan",
    )(x, g, w_cs)


def _fnet_seq_kernel(cl_ref, sl_ref, p_ref, q_ref, o_ref):
    acc = jnp.dot(cl_ref[...], p_ref[...], preferred_element_type=F32)
    acc += jnp.dot(sl_ref[...], q_ref[...], preferred_element_type=F32)
    o_ref[...] = acc.astype(o_ref.dtype)


def fnet_seq(cl, sl_neg, pq, *, tm, tn):
    t = pq.shape[0]
    d = pq.shape[1] // 2
    nb = t // SEQ
    ni = SEQ // tm
    nj = d // tn
    return pl.pallas_call(
        _fnet_seq_kernel,
        grid=(nb, nj, ni),
        in_specs=[
            pl.BlockSpec((tm, SEQ), lambda b, j, i: (i, 0)),
            pl.BlockSpec((tm, SEQ), lambda b, j, i: (i, 0)),
            pl.BlockSpec((SEQ, tn), lambda b, j, i: (b, j)),
            pl.BlockSpec((SEQ, tn), lambda b, j, i: (b, nj + j)),
        ],
        out_specs=pl.BlockSpec((tm, tn), lambda b, j, i: (b * ni + i, j)),
        out_shape=jax.ShapeDtypeStruct((t, d), BF16),
        compiler_params=_params("parallel", "parallel", "arbitrary"),
        name="fnet_seq",
    )(cl, sl_neg, pq, pq)


def _dft_tables():
    def cos_sin(n):
        idx = jnp.arange(n, dtype=jnp.int32)
        ang = ((idx[:, None] * idx[None, :]) % n).astype(F32) * (2.0 * np.pi / n)
        return jnp.cos(ang), jnp.sin(ang)

    cc, sc = cos_sin(FNET_WIDTH)
    cl, sl = cos_sin(SEQ)
    scale = 1.0 / np.sqrt(SEQ * FNET_WIDTH)
    w_cs = jnp.stack([cc, sc]).astype(BF16)
    return w_cs, (cl * scale).astype(BF16), (-sl * scale).astype(BF16)


ATTN_PAIR = 2 * ATTN_HEAD_DIM
ATTN_KEYS = 3 * WINDOW
ATTN_Q_TILE = 2 * ATTN_GROUP * ATTN_HEAD_DIM


def _rope(v, tc, ta, tb):
    n = v.shape[-1]
    half = ROPE_DIM // 2
    return v * tc + pltpu.roll(v, n - half, 1) * ta + pltpu.roll(v, half, 1) * tb


def _attn_kernel(q_ref, k_ref, v_ref, tcq_ref, taq_ref, tbq_ref, tck_ref, tak_ref, tbk_ref, sink_ref,
                 o_ref, kdup_ref, vlo_ref, vhi_ref):
    n = pl.program_id(2)
    n_kv = 2

    @pl.when(n == 0)
    def _():
        kr = _rope(k_ref[...], tck_ref[...], tak_ref[...], tbk_ref[...])
        vv = v_ref[...]
        kr_sw = pltpu.roll(kr, ATTN_HEAD_DIM, 1)
        vv_sw = pltpu.roll(vv, ATTN_HEAD_DIM, 1)
        lo = lax.broadcasted_iota(jnp.int32, (SEQ, ATTN_PAIR), 1) < ATTN_HEAD_DIM
        kdup_ref[0] = jnp.where(lo, kr, kr_sw).astype(BF16)
        kdup_ref[1] = jnp.where(lo, kr_sw, kr).astype(BF16)
        ones = jnp.ones((SEQ, ATTN_PAIR), BF16)
        vlo_ref[0] = jnp.concatenate([jnp.where(lo, vv, 0.0).astype(BF16), ones], axis=1)
        vlo_ref[1] = jnp.concatenate([jnp.where(lo, vv_sw, 0.0).astype(BF16), ones], axis=1)
        vhi_ref[0] = jnp.concatenate([jnp.where(lo, 0.0, vv_sw).astype(BF16), ones], axis=1)
        vhi_ref[1] = jnp.concatenate([jnp.where(lo, 0.0, vv).astype(BF16), ones], axis=1)

    start = pl.multiple_of(jnp.clip((n - 1) * WINDOW, 0, SEQ - ATTN_KEYS), WINDOW)
    keys = pl.ds(start, ATTN_KEYS)
    n_stack = 2 * (ATTN_GROUP // 2)
    qpos = n * WINDOW + lax.broadcasted_iota(jnp.int32, (n_stack, WINDOW, ATTN_KEYS), 1)
    kpos = start + lax.broadcasted_iota(jnp.int32, (n_stack, WINDOW, ATTN_KEYS), 2)
    valid = (jnp.abs(qpos - kpos) <= WINDOW).reshape(n_stack * WINDOW, ATTN_KEYS)
    q_lo = lax.broadcasted_iota(jnp.int32, (WINDOW, ATTN_PAIR), 1) < ATTN_HEAD_DIM
    q_scale = (ATTN_HEAD_DIM ** -0.5) * LOG2_E

    half_rows = (ATTN_GROUP // 2) * WINDOW
    pairs_of = [[kv * (ATTN_GROUP // 2) + i for i in range(ATTN_GROUP // 2)] for kv in range(n_kv)]
    scores, sinks = [], []
    for kv in range(n_kv):
        pairs = pairs_of[kv]
        qs = [_rope(q_ref[:, p * ATTN_PAIR:(p + 1) * ATTN_PAIR], tcq_ref[...], taq_ref[...], tbq_ref[...]) * q_scale
              for p in pairs]
        lhs = jnp.concatenate([jnp.where(q_lo, q, 0.0) for q in qs] + [jnp.where(q_lo, 0.0, q) for q in qs],
                              axis=0).astype(BF16)
        sinks.append(jnp.concatenate(
            [jnp.broadcast_to(sink_ref[2 * p + half:2 * p + half + 1, :], (WINDOW, ATTN_PAIR))
             for half in range(2) for p in pairs], axis=0) * LOG2_E)
        s = lax.dot_general(lhs, kdup_ref[kv, keys, :], (((1,), (1,)), ((), ())), preferred_element_type=F32)
        scores.append(jnp.where(valid, s, -jnp.inf))
    probs, e_sinks = [], []
    for kv in range(n_kv):
        s, sink = scores[kv], sinks[kv]
        m = jnp.maximum(jnp.broadcast_to(jnp.max(s, axis=-1, keepdims=True), sink.shape), sink)
        probs.append(jnp.exp2(s - jnp.tile(m, (1, ATTN_KEYS // ATTN_PAIR))).astype(BF16))
        e_sinks.append(jnp.exp2(sink - m))
    for kv in range(n_kv):
        p_un, e_sink = probs[kv], e_sinks[kv]
        pv_lo = jnp.dot(p_un[:half_rows], vlo_ref[kv, keys, :], preferred_element_type=F32)
        pv_hi = jnp.dot(p_un[half_rows:], vhi_ref[kv, keys, :], preferred_element_type=F32)
        out = (pv_lo[:, :ATTN_PAIR] / (pv_lo[:, ATTN_PAIR:] + e_sink[:half_rows])
               + pv_hi[:, :ATTN_PAIR] / (pv_hi[:, ATTN_PAIR:] + e_sink[half_rows:]))
        for i, p in enumerate(pairs_of[kv]):
            o_ref[:, p * ATTN_PAIR:(p + 1) * ATTN_PAIR] = out[i * WINDOW:(i + 1) * WINDOW].astype(o_ref.dtype)


def window_attention(qkv, tables, sinks):
    t = qkv.shape[0]
    nb = t // SEQ
    nblk = SEQ // WINDOW
    n_pairs = ATTN_KV_HEADS // 2
    k_col0 = ATTN_Q_HEADS * ATTN_HEAD_DIM // ATTN_PAIR
    v_col0 = k_col0 + n_pairs
    tc, ta, tb = tables
    q_tab = pl.BlockSpec((WINDOW, ATTN_PAIR), lambda b, kp, n: (n, 0))
    k_tab = pl.BlockSpec((SEQ, ATTN_PAIR), lambda b, kp, n: (0, 0))
    return pl.pallas_call(
        _attn_kernel,
        grid=(nb, n_pairs, nblk),
        in_specs=[
            pl.BlockSpec((WINDOW, ATTN_Q_TILE), lambda b, kp, n: (b * nblk + n, kp)),
            pl.BlockSpec((SEQ, ATTN_PAIR), lambda b, kp, n: (b, k_col0 + kp)),
            pl.BlockSpec((SEQ, ATTN_PAIR), lambda b, kp, n: (b, v_col0 + kp)),
            q_tab, q_tab, q_tab, k_tab, k_tab, k_tab,
            pl.BlockSpec((None, 2 * ATTN_GROUP, 1), lambda b, kp, n: (kp, 0, 0)),
        ],
        out_specs=pl.BlockSpec((WINDOW, ATTN_Q_TILE), lambda b, kp, n: (b * nblk + n, kp)),
        out_shape=jax.ShapeDtypeStruct((t, ATTN_Q_HEADS * ATTN_HEAD_DIM), BF16),
        scratch_shapes=[pltpu.VMEM((2, SEQ, ATTN_PAIR), BF16)] + [pltpu.VMEM((2, SEQ, 2 * ATTN_PAIR), BF16)] * 2,
        compiler_params=_params("parallel", "parallel", "arbitrary"),
        name="window_attention",
    )(qkv, qkv, qkv, tc, ta, tb, tc, ta, tb, sinks.reshape(n_pairs, 2 * ATTN_GROUP, 1))


def _rope_lane_tables():
    half = ROPE_DIM // 2
    inv = 1.0 / (ROPE_THETA ** (jnp.arange(0, ROPE_DIM, 2, dtype=F32) / ROPE_DIM))
    ang = jnp.arange(SEQ, dtype=F32)[:, None] * inv[None, :]
    cos, sin = jnp.cos(ang), jnp.sin(ang)
    pad = ATTN_HEAD_DIM - ROPE_DIM
    tc = jnp.concatenate([cos, cos, jnp.ones((SEQ, pad), F32)], axis=1)
    ta = jnp.concatenate([-sin, jnp.zeros((SEQ, half + pad), F32)], axis=1)
    tb = jnp.concatenate([jnp.zeros((SEQ, half), F32), sin, jnp.zeros((SEQ, pad), F32)], axis=1)
    return tuple(jnp.tile(tab, (1, 2)) for tab in (tc, ta, tb))


ROW_TILE = 1024
FFN_TILE = 512


def _ssd_layer(x, g_pre, g_post, w_in, j, conv_w, conv_b, dt_bias, a_log, d_skip, norm_w, w_out):
    proj, dt_raw = norm_matmul(x, g_pre, w_in, j, tm=ROW_TILE, tn=1024, n_main=SSD_MAIN_DIM, out_dtype=BF16)
    small = _ssd_small_params(dt_bias, a_log, d_skip)
    y = ssd_scan(proj, _ssd_dt_layout(dt_raw), conv_w, conv_b[None, :], *small, norm_w[None, :])
    return matmul_norm_res(y, w_out, j, g_post, x, tm=ROW_TILE, tk=1024)


def _fnet_layer(x, g_pre, g_post, w_out, j, dft):
    w_cs, cl, sl_neg = dft
    pq = fnet_chan(x, g_pre, w_cs, tm=ROW_TILE)
    f = fnet_seq(cl, sl_neg, pq, tm=1024, tn=1024)
    return matmul_norm_res(f, w_out, j, g_post, x, tm=ROW_TILE, tk=D_MODEL)


def _attn_layer(x, g_pre, g_post, w_in, j, sinks, w_out, rope):
    qkv = norm_matmul(x, g_pre, w_in, j, tm=ROW_TILE, tn=1024, n_main=ATTN_IN_DIM, out_dtype=F32)
    a = window_attention(qkv, rope, sinks)
    return matmul_norm_res(a, w_out, j, g_post, x, tm=ROW_TILE, tk=D_MODEL)


def kernel(x_prompt, x_sample, norm_w, ffn_w_in, ffn_w_out, ssd_w_in, ssd_conv_w, ssd_conv_b, ssd_dt_bias,
           ssd_a_log, ssd_d, ssd_norm_w, ssd_w_out, fnet_w_out, attn_w_in, attn_sinks, attn_w_out):
    n_prompt = x_prompt.shape[0]
    assert x_prompt.shape[1:] == (SEQ, D_MODEL) and x_sample.shape[1:] == (SEQ, D_MODEL)
    x = jnp.concatenate([x_prompt, x_sample], axis=0).reshape(-1, D_MODEL)
    dft = _dft_tables()
    rope = _rope_lane_tables()
    ffn_w_in, ffn_w_out, ssd_w_in, ssd_w_out, fnet_w_out, attn_w_in, attn_w_out = (
        w.astype(BF16) for w in (ffn_w_in, ffn_w_out, ssd_w_in, ssd_w_out, fnet_w_out, attn_w_in, attn_w_out))
    for i in range(DEPTH):
        kind, j = i % N_MIXERS, i // N_MIXERS
        g = norm_w[i][:, None, :]
        if kind == 0:
            x = _ssd_layer(x, g[0], g[1], ssd_w_in, j, ssd_conv_w[j], ssd_conv_b[j], ssd_dt_bias[j],
                           ssd_a_log[j], ssd_d[j], ssd_norm_w[j], ssd_w_out)
        elif kind == 1:
            x = _fnet_layer(x, g[0], g[1], fnet_w_out, j, dft)
        else:
            x = _attn_layer(x, g[0], g[1], attn_w_in, j, attn_sinks[j], attn_w_out, rope)
        if i < DEPTH - 1:
            x = ffn(x, g[2], ffn_w_in, ffn_w_out, i, g[3], tm=ROW_TILE, tf=FFN_TILE)
    split = n_prompt * SEQ
    last = DEPTH - 1
    g = norm_w[last][:, None, :]
    y_prompt = ffn(x, g[2], ffn_w_in, ffn_w_out, last, g[3], tm=ROW_TILE, tf=FFN_TILE, rows=(0, split))
    y_sample = ffn(x, g[2], ffn_w_in, ffn_w_out, last, g[3], tm=ROW_TILE, tf=FFN_TILE, rows=(split, x.shape[0]))
    return (y_prompt.reshape(-1, SEQ, D_MODEL), y_sample.reshape(-1, SEQ, D_MODEL))
```
